```python
import jax, jax.numpy as jnp
from jax import lax
import numpy as np

D_MODEL = 1024
BATCH = 32
SEQ = 2048
DEPTH = 1

D_LRU = D_MODEL
LRU_BLOCKS = 16
LRU_BLOCK_DIM = D_LRU // LRU_BLOCKS
CONV_WIDTH = 4
LRU_C = 8.0
GLA_HEADS = 4
GLA_QK_DIM = D_MODEL // 2
GLA_V_DIM = D_MODEL
GLA_DK = GLA_QK_DIM // GLA_HEADS
GLA_DV = GLA_V_DIM // GLA_HEADS
GLA_GATE_RANK = 16
GLA_TAU = 16.0
GLA_CHUNK = 64
N_EXPERTS = 256
TOP_K = 8
N_GROUPS = 8
TOPK_GROUPS = 4
D_EXPERT = 256
D_SHARED = 256
ROUTED_SCALE = 2.5
MOE_BLOCK = 128
DN_ALPHA = (2.0 * DEPTH) ** 0.25
DN_BETA = (8.0 * DEPTH) ** -0.25
LN_EPS = 1e-5
RMS_EPS = 1e-6

SPLIT_SIZES = (D_LRU, D_LRU, GLA_QK_DIM, GLA_QK_DIM, GLA_V_DIM, GLA_V_DIM, GLA_GATE_RANK, D_MODEL, D_MODEL)
D_PROJ = sum(SPLIT_SIZES)
SPLIT_POINTS = tuple(sum(SPLIT_SIZES[:i + 1]) for i in range(len(SPLIT_SIZES) - 1))

kernel_name = 'hybrid_rglru_gla_moe_deepnorm'


def layer_norm(x, g, b):
    xf = x.astype(jnp.float32)
    mu = jnp.mean(xf, axis=-1, keepdims=True)
    var = jnp.mean(jnp.square(xf - mu), axis=-1, keepdims=True)
    y = (xf - mu) * lax.rsqrt(var + LN_EPS) * g.astype(jnp.float32) + b.astype(jnp.float32)
    return y.astype(x.dtype)


def linear_scan(a, u):
    def step(h_prev, au):
        a_t, u_t = au
        h_t = a_t * h_prev + u_t
        return h_t, h_t
    h0 = jnp.zeros((a.shape[0], a.shape[2]), a.dtype)
    _, hs = lax.scan(step, h0, (jnp.moveaxis(a, 1, 0), jnp.moveaxis(u, 1, 0)))
    return jnp.moveaxis(hs, 0, 1)


def rglru_branch(xa, ga, conv_w, conv_b, lru_w_a, lru_b_a, lru_w_x, lru_b_x, lru_lambda):
    b_, s_, _ = xa.shape
    xc = lax.conv_general_dilated(xa, conv_w[:, None, :], window_strides=(1,),
                                  padding=((CONV_WIDTH - 1, 0),),
                                  dimension_numbers=('NWC', 'WIO', 'NWC'),
                                  feature_group_count=D_LRU) + conv_b
    xb = xc.reshape(b_, s_, LRU_BLOCKS, LRU_BLOCK_DIM)
    r = jax.nn.sigmoid(jnp.einsum('bsgi,gij->bsgj', xb, lru_w_a).reshape(b_, s_, D_LRU) + lru_b_a)
    i = jax.nn.sigmoid(jnp.einsum('bsgi,gij->bsgj', xb, lru_w_x).reshape(b_, s_, D_LRU) + lru_b_x)
    log_a = (-LRU_C * r * jax.nn.softplus(-lru_lambda)).astype(jnp.float32)
    a = jnp.exp(log_a)
    u = jnp.sqrt(-jnp.expm1(2.0 * log_a)) * (i * xc).astype(jnp.float32)
    h = linear_scan(a, u)
    return h.astype(xa.dtype) * jax.nn.gelu(ga)


def gla_chunked(q, k, v, log_a):
    b_, s_, h_, dk = q.shape
    dv = v.shape[-1]
    n_ch = s_ // GLA_CHUNK
    def to_chunks(t):
        return t.reshape(b_, n_ch, GLA_CHUNK, h_, t.shape[-1])
    q, k, v, log_a = to_chunks(q), to_chunks(k), to_chunks(v), to_chunks(log_a)
    bcum = jnp.cumsum(log_a, axis=2)
    b_last = bcum[:, :, -1:]
    q_dec = q * jnp.exp(bcum)
    k_inv = k * jnp.exp(-bcum)
    k_end = k * jnp.exp(b_last - bcum)
    causal = jnp.tril(jnp.ones((GLA_CHUNK, GLA_CHUNK), dtype=bool))
    scores = jnp.einsum('bnihk,bnjhk->bnhij', q_dec, k_inv)
    scores = jnp.where(causal, scores, 0.0)
    o_intra = jnp.einsum('bnhij,bnjhv->bnihv', scores, v)

    def step(state, inp):
        q_n, k_n, v_n, g_n = inp
        o_n = jnp.einsum('bchk,bhkv->bchv', q_n, state)
        state = g_n[..., None] * state + jnp.einsum('bchk,bchv->bhkv', k_n, v_n)
        return state, o_n
    xs = (jnp.moveaxis(q_dec, 1, 0), jnp.moveaxis(k_end, 1, 0), jnp.moveaxis(v, 1, 0),
          jnp.moveaxis(jnp.exp(b_last[:, :, 0]), 1, 0))
    state0 = jnp.zeros((b_, h_, dk, dv), jnp.float32)
    _, o_inter = lax.scan(step, state0, xs)
    o = o_intra + jnp.moveaxis(o_inter, 0, 1)
    return o.reshape(b_, s_, h_, dv)


def gla_branch(q, k, v, r, g_low, gla_w_gate_up, gla_b_gate, gla_norm_g):
    b_, s_, _ = q.shape
    f32 = jnp.float32
    qh = q.reshape(b_, s_, GLA_HEADS, GLA_DK).astype(f32) * (GLA_DK ** -0.5)
    kh = k.reshape(b_, s_, GLA_HEADS, GLA_DK).astype(f32)
    vh = v.reshape(b_, s_, GLA_HEADS, GLA_DV).astype(f32)
    gate_logit = (jnp.einsum('bsr,rk->bsk', g_low, gla_w_gate_up) + gla_b_gate).astype(f32)
    log_a = (jax.nn.log_sigmoid(gate_logit) / GLA_TAU).reshape(b_, s_, GLA_HEADS, GLA_DK)
    o = gla_chunked(qh, kh, vh, log_a)
    o = o * lax.rsqrt(jnp.mean(jnp.square(o), axis=-1, keepdims=True) + RMS_EPS)
    o = o * gla_norm_g.astype(f32).reshape(GLA_HEADS, GLA_DV)
    return o.reshape(b_, s_, GLA_V_DIM).astype(q.dtype) * jax.nn.silu(r)


def token_mixer(x, w_in, conv_w, conv_b, lru_w_a, lru_b_a, lru_w_x, lru_b_x, lru_lambda, w_lru_out,
                gla_w_gate_up, gla_b_gate, gla_norm_g, w_gla_out, w_mix_out):
    p = jnp.einsum('bsd,de->bse', x, w_in)
    xa, ga, q, k, v, r, g_low, gate_a, gate_b = jnp.split(p, SPLIT_POINTS, axis=-1)
    y_a = jnp.einsum('bsc,cd->bsd',
                     rglru_branch(xa, ga, conv_w, conv_b, lru_w_a, lru_b_a, lru_w_x, lru_b_x, lru_lambda),
                     w_lru_out)
    y_b = jnp.einsum('bsc,cd->bsd',
                     gla_branch(q, k, v, r, g_low, gla_w_gate_up, gla_b_gate, gla_norm_g),
                     w_gla_out)
    merged = jax.nn.sigmoid(gate_a) * y_a + jax.nn.sigmoid(gate_b) * y_b
    return jnp.einsum('bsd,de->bse', merged, w_mix_out)


def swiglu(xt, w_gate, w_up, w_down):
    return (jax.nn.silu(xt @ w_gate) * (xt @ w_up)) @ w_down


def moe(h, w_router, router_bias, w_exp_gate, w_exp_up, w_exp_down, w_sh_gate, w_sh_up, w_sh_down):
    b_, s_, d_ = h.shape
    n_tok = b_ * s_
    hf = h.reshape(n_tok, d_)
    scores = jax.nn.sigmoid((hf @ w_router).astype(jnp.float32))
    biased = scores + router_bias.astype(jnp.float32)
    grp_score = lax.top_k(biased.reshape(n_tok, N_GROUPS, N_EXPERTS // N_GROUPS), 2)[0].sum(-1)
    _, top_g = lax.top_k(grp_score, TOPK_GROUPS)
    gmask = jnp.any(top_g[..., None] == jnp.arange(N_GROUPS), axis=1)
    emask = jnp.repeat(gmask, N_EXPERTS // N_GROUPS, axis=1)
    _, idx = lax.top_k(jnp.where(emask, biased, -jnp.inf), TOP_K)
    w = jnp.take_along_axis(scores, idx, axis=1)
    w = w / jnp.sum(w, axis=-1, keepdims=True) * ROUTED_SCALE

    n_asg = n_tok * TOP_K
    n_blocks = (n_asg + MOE_BLOCK - 1) // MOE_BLOCK + N_EXPERTS
    e_flat = idx.reshape(-1)
    tok_flat = jnp.repeat(jnp.arange(n_tok, dtype=jnp.int32), TOP_K)
    w_flat = w.reshape(-1)
    order = jnp.argsort(e_flat)
    e_s, tok_s, w_s = e_flat[order], tok_flat[order], w_flat[order]
    counts = jax.ops.segment_sum(jnp.ones_like(e_flat), e_flat, num_segments=N_EXPERTS)
    padded = (counts + MOE_BLOCK - 1) // MOE_BLOCK * MOE_BLOCK
    start = jnp.cumsum(counts) - counts
    pend = jnp.cumsum(padded)
    pstart = pend - padded
    dest = pstart[e_s] + (jnp.arange(n_asg, dtype=jnp.int32) - start[e_s])
    slot_tok = jnp.full((n_blocks * MOE_BLOCK,), n_tok, jnp.int32).at[dest].set(tok_s)
    slot_w = jnp.zeros((n_blocks * MOE_BLOCK,), jnp.float32).at[dest].set(w_s)
    block_expert = jnp.minimum(
        jnp.searchsorted(pend, jnp.arange(n_blocks, dtype=jnp.int32) * MOE_BLOCK, side='right'),
        N_EXPERTS - 1)
    h_pad = jnp.concatenate([hf, jnp.zeros((1, d_), hf.dtype)], axis=0)

    def body(acc, blk):
        toks, wts, e = blk
        xb = h_pad[toks]
        y = swiglu(xb, w_exp_gate[e], w_exp_up[e], w_exp_down[e])
        return acc.at[toks].add(y * wts[:, None].astype(y.dtype)), None
    acc0 = jnp.zeros((n_tok + 1, d_), hf.dtype)
    acc, _ = lax.scan(body, acc0, (slot_tok.reshape(n_blocks, MOE_BLOCK),
                                   slot_w.reshape(n_blocks, MOE_BLOCK), block_expert))
    out = acc[:n_tok] + swiglu(hf, w_sh_gate, w_sh_up, w_sh_down)
    return out.reshape(b_, s_, d_)


def hybrid_layer(x, w_in, conv_w, conv_b, lru_w_a, lru_b_a, lru_w_x, lru_b_x, lru_lambda, w_lru_out,
                 gla_w_gate_up, gla_b_gate, gla_norm_g, w_gla_out, w_mix_out, ln1_g, ln1_b,
                 w_router, router_bias, w_exp_gate, w_exp_up, w_exp_down, w_sh_gate, w_sh_up, w_sh_down,
                 ln2_g, ln2_b):
    mix = token_mixer(x, w_in, conv_w, conv_b, lru_w_a, lru_b_a, lru_w_x, lru_b_x, lru_lambda, w_lru_out,
                      gla_w_gate_up, gla_b_gate, gla_norm_g, w_gla_out, w_mix_out)
    h = layer_norm(DN_ALPHA * x + mix, ln1_g, ln1_b)
    ffn = moe(h, w_router, router_bias, w_exp_gate, w_exp_up, w_exp_down, w_sh_gate, w_sh_up, w_sh_down)
    return layer_norm(DN_ALPHA * h + ffn, ln2_g, ln2_b)


def setup_inputs(seed: int = 0) -> dict:
    key = jax.random.key(seed)
    ks = jax.random.split(key, 32)
    L = DEPTH
    f32 = jnp.float32

    def nrm(k, shape, fan_in, scale=1.0):
        return jax.random.normal(k, shape, f32) * (scale * fan_in ** -0.5)

    def gain(k, n):
        return 1.0 + 0.05 * jax.random.normal(k, (L, n), f32)

    def bias(k, n, s=0.02):
        return s * jax.random.normal(k, (L, n), f32)

    u = jax.random.uniform(ks[8], (L, D_LRU), f32, minval=0.9, maxval=0.999)
    a0 = u ** (1.0 / LRU_C)
    lru_lambda = jnp.log(a0) - jnp.log1p(-a0)
    return {
        'x': jax.random.normal(ks[0], (BATCH, SEQ, D_MODEL), f32),
        'w_in': nrm(ks[1], (L, D_MODEL, D_PROJ), D_MODEL),
        'conv_w': nrm(ks[2], (L, CONV_WIDTH, D_LRU), CONV_WIDTH),
        'conv_b': bias(ks[3], D_LRU),
        'lru_w_a': nrm(ks[4], (L, LRU_BLOCKS, LRU_BLOCK_DIM, LRU_BLOCK_DIM), LRU_BLOCK_DIM),
        'lru_b_a': bias(ks[5], D_LRU),
        'lru_w_x': nrm(ks[6], (L, LRU_BLOCKS, LRU_BLOCK_DIM, LRU_BLOCK_DIM), LRU_BLOCK_DIM),
        'lru_b_x': bias(ks[7], D_LRU),
        'lru_lambda': lru_lambda,
        'w_lru_out': nrm(ks[9], (L, D_LRU, D_MODEL), D_LRU, DN_BETA),
        'gla_w_gate_up': nrm(ks[10], (L, GLA_GATE_RANK, GLA_QK_DIM), GLA_GATE_RANK),
        'gla_b_gate': bias(ks[11], GLA_QK_DIM, 0.1),
        'gla_norm_g': gain(ks[12], GLA_V_DIM),
        'w_gla_out': nrm(ks[13], (L, GLA_V_DIM, D_MODEL), GLA_V_DIM, DN_BETA),
        'w_mix_out': nrm(ks[14], (L, D_MODEL, D_MODEL), D_MODEL, DN_BETA),
        'ln1_g': gain(ks[15], D_MODEL),
        'ln1_b': bias(ks[16], D_MODEL),
        'w_router': nrm(ks[17], (L, D_MODEL, N_EXPERTS), D_MODEL),
        'router_bias': bias(ks[18], N_EXPERTS, 0.01),
        'w_exp_gate': nrm(ks[19], (L, N_EXPERTS, D_MODEL, D_EXPERT), D_MODEL),
        'w_exp_up': nrm(ks[20], (L, N_EXPERTS, D_MODEL, D_EXPERT), D_MODEL),
        'w_exp_down': nrm(ks[21], (L, N_EXPERTS, D_EXPERT, D_MODEL), D_EXPERT, DN_BETA),
        'w_sh_gate': nrm(ks[22], (L, D_MODEL, D_SHARED), D_MODEL),
        'w_sh_up': nrm(ks[23], (L, D_MODEL, D_SHARED), D_MODEL),
        'w_sh_down': nrm(ks[24], (L, D_SHARED, D_MODEL), D_SHARED, DN_BETA),
        'ln2_g': gain(ks[25], D_MODEL),
        'ln2_b': bias(ks[26], D_MODEL),
    }


def reference(x, w_in, conv_w, conv_b, lru_w_a, lru_b_a, lru_w_x, lru_b_x, lru_lambda, w_lru_out,
              gla_w_gate_up, gla_b_gate, gla_norm_g, w_gla_out, w_mix_out, ln1_g, ln1_b,
              w_router, router_bias, w_exp_gate, w_exp_up, w_exp_down, w_sh_gate, w_sh_up, w_sh_down,
              ln2_g, ln2_b):
    layer_params = (w_in, conv_w, conv_b, lru_w_a, lru_b_a, lru_w_x, lru_b_x, lru_lambda, w_lru_out,
                    gla_w_gate_up, gla_b_gate, gla_norm_g, w_gla_out, w_mix_out, ln1_g, ln1_b,
                    w_router, router_bias, w_exp_gate, w_exp_up, w_exp_down, w_sh_gate, w_sh_up, w_sh_down,
                    ln2_g, ln2_b)
    h = x
    for l in range(DEPTH):
        h = hybrid_layer(h, *(p[l] for p in layer_params))
    return h
```

```python
import functools

import jax
import jax.numpy as jnp
from jax import lax
from jax.experimental import pallas as pl
from jax.experimental.pallas import tpu as pltpu

F32 = jnp.float32
BF16 = jnp.bfloat16
I32 = jnp.int32

D_MODEL = 1024
D_LRU = D_MODEL
LRU_BLOCKS = 16
LRU_BLOCK_DIM = D_LRU // LRU_BLOCKS
CONV_WIDTH = 4
LRU_C = 8.0
GLA_HEADS = 4
GLA_QK_DIM = D_MODEL // 2
GLA_V_DIM = D_MODEL
GLA_DK = GLA_QK_DIM // GLA_HEADS
GLA_DV = GLA_V_DIM // GLA_HEADS
GLA_GATE_RANK = 16
GLA_TAU = 16.0
GLA_CHUNK = 64
N_EXPERTS = 256
TOP_K = 8
N_GROUPS = 8
GROUP_SIZE = N_EXPERTS // N_GROUPS
TOPK_GROUPS = 4
D_EXPERT = 256
D_SHARED = 256
ROUTED_SCALE = 2.5
DEPTH = 1
DN_ALPHA = (2.0 * DEPTH) ** 0.25
LN_EPS = 1e-5
RMS_EPS = 1e-6

LANES = 128
SUBLANES = 8
GLOW_PAD = LANES
N_MAIN = 2 * D_LRU + 2 * GLA_QK_DIM + 2 * GLA_V_DIM + 2 * D_MODEL
N_PROJ = N_MAIN + GLOW_PAD
PROJ_CHUNK = 512

EXPERT_ROWS = 256


def _cparams(semantics, vmem_mib):
    return pltpu.CompilerParams(dimension_semantics=semantics,
                                vmem_limit_bytes=vmem_mib * 1024 * 1024)


def _sigmoid(x):
    return jax.nn.sigmoid(x)


def _silu(x):
    return x * jax.nn.sigmoid(x)


def _gelu_tanh(x):
    c = 0.7978845608028654
    return x * (0.5 * (1.0 + jnp.tanh(c * (x + 0.044715 * (x * x * x)))))


def _layer_norm(z, g, b):
    mu = jnp.mean(z, axis=-1, keepdims=True)
    zc = z - mu
    var = jnp.mean(zc * zc, axis=-1, keepdims=True)
    return zc * lax.rsqrt(var + LN_EPS) * g + b


def _proj_kernel(x_ref, w_ref, o_ref):
    xb = x_ref[...].astype(BF16)
    for j in range(0, N_MAIN, PROJ_CHUNK):
        o_ref[:, j:j + PROJ_CHUNK] = jnp.dot(
            xb, w_ref[:, j:j + PROJ_CHUNK], preferred_element_type=F32).astype(BF16)
    o_ref[:, N_MAIN:] = jnp.dot(xb, w_ref[:, N_MAIN:], preferred_element_type=F32).astype(BF16)


def _proj(x2, w_p, tm):
    t, d = x2.shape
    return pl.pallas_call(
        _proj_kernel,
        grid=(t // tm,),
        in_specs=[pl.BlockSpec((tm, d), lambda i: (i, 0)),
                  pl.BlockSpec((d, N_PROJ), lambda i: (0, 0), pipeline_mode=pl.Buffered(1))],
        out_specs=pl.BlockSpec((tm, N_PROJ), lambda i: (i, 0)),
        out_shape=jax.ShapeDtypeStruct((t, N_PROJ), BF16),
        compiler_params=_cparams(("arbitrary",), 48),
        name="proj",
    )(x2, w_p)


def _lru_kernel(xa_ref, ga_ref, cw_ref, cb_ref, wg_ref, ba_ref, bx_ref, lam_ref, o_ref,
                xbuf, a_s, u_s, cin_s, carry, *, ts):
    nb = ts // SUBLANES
    nslab = D_LRU // LANES
    grp = 4 * LRU_BLOCK_DIM

    @pl.when(pl.program_id(1) == 0)
    def _init():
        xbuf[0:SUBLANES, :] = jnp.zeros((SUBLANES, D_LRU), F32)
        carry[...] = jnp.zeros((nslab, LANES), F32)

    xbuf[SUBLANES:, :] = xa_ref[...].astype(F32)
    xc = cb_ref[...]
    for j in range(CONV_WIDTH):
        lo = SUBLANES - (CONV_WIDTH - 1) + j
        xc = xc + cw_ref[j:j + 1, :] * xbuf[lo:lo + ts, :]
    xbuf[0:SUBLANES, :] = xbuf[ts:ts + SUBLANES, :]

    lam = lam_ref[...]
    sp = jnp.maximum(-lam, 0.0) + jnp.log1p(jnp.exp(-jnp.abs(lam)))
    for g in range(D_LRU // grp):
        sl = slice(g * grp, (g + 1) * grp)
        xg = xc[:, sl]
        z = jnp.dot(xg.astype(BF16), wg_ref[g], preferred_element_type=F32)
        r = _sigmoid(z[:, :grp] + ba_ref[:, sl])
        i = _sigmoid(z[:, grp:] + bx_ref[:, sl])
        log_a = (-LRU_C * r) * sp[:, sl]
        a = jnp.exp(log_a)
        u = jnp.sqrt(-jnp.tanh(log_a) * (a * a + 1.0)) * (i * xg)
        for h in range(grp // LANES):
            s = g * (grp // LANES) + h
            a_s[s] = a[:, h * LANES:(h + 1) * LANES]
            u_s[s] = u[:, h * LANES:(h + 1) * LANES]

    for s in range(nslab):
        hh = u_s[s, pl.ds(0, nb, stride=SUBLANES), :]
        aa = a_s[s, pl.ds(0, nb, stride=SUBLANES), :]
        for r in range(1, SUBLANES):
            a_r = a_s[s, pl.ds(r, nb, stride=SUBLANES), :]
            hh = a_r * hh + u_s[s, pl.ds(r, nb, stride=SUBLANES), :]
            aa = a_r * aa
            u_s[s, pl.ds(r, nb, stride=SUBLANES), :] = hh
            a_s[s, pl.ds(r, nb, stride=SUBLANES), :] = aa

    def _carry_step(b, cs):
        last = b * SUBLANES + (SUBLANES - 1)
        out = []
        for s in range(nslab):
            cin_s[s, pl.ds(b, 1), :] = cs[s]
            out.append(a_s[s, pl.ds(last, 1), :] * cs[s] + u_s[s, pl.ds(last, 1), :])
        return tuple(out)

    c0 = tuple(carry[s:s + 1, :] for s in range(nslab))
    cs = lax.fori_loop(0, nb, _carry_step, c0)
    for s in range(nslab):
        carry[s:s + 1, :] = cs[s]

    for s in range(nslab):
        cin = cin_s[s]
        for r in range(SUBLANES):
            hloc = u_s[s, pl.ds(r, nb, stride=SUBLANES), :]
            acum = a_s[s, pl.ds(r, nb, stride=SUBLANES), :]
            u_s[s, pl.ds(r, nb, stride=SUBLANES), :] = hloc + acum * cin

    for s in range(nslab):
        cs_ = slice(s * LANES, (s + 1) * LANES)
        o_ref[:, cs_] = (u_s[s] * _gelu_tanh(ga_ref[:, cs_].astype(F32))).astype(BF16)


def _lru(p, cw, cb, wg, ba, bx, lam, batch, seq, ts):
    t = batch * seq
    nst = seq // ts
    nslab = D_LRU // LANES
    row = lambda b, s: b * nst + s
    full = lambda shape: pl.BlockSpec(shape, lambda b, s: (0,) * len(shape))
    return pl.pallas_call(
        functools.partial(_lru_kernel, ts=ts),
        grid=(batch, nst),
        in_specs=[pl.BlockSpec((ts, D_LRU), lambda b, s: (row(b, s), 0)),
                  pl.BlockSpec((ts, D_LRU), lambda b, s: (row(b, s), 1)),
                  full((CONV_WIDTH, D_LRU)), full((1, D_LRU)),
                  full(wg.shape), full((1, D_LRU)), full((1, D_LRU)), full((1, D_LRU))],
        out_specs=pl.BlockSpec((ts, D_LRU), lambda b, s: (row(b, s), 0)),
        out_shape=jax.ShapeDtypeStruct((t, D_LRU), BF16),
        scratch_shapes=[pltpu.VMEM((ts + SUBLANES, D_LRU), F32),
                        pltpu.VMEM((nslab, ts, LANES), F32),
                        pltpu.VMEM((nslab, ts, LANES), F32),
                        pltpu.VMEM((nslab, ts // SUBLANES, LANES), F32),
                        pltpu.VMEM((nslab, LANES), F32)],
        compiler_params=_cparams(("arbitrary", "arbitrary"), 40),
        name="lru",
    )(p, p, cw, cb, wg, ba, bx, lam)


def _gla_kernel(q_ref, k_ref, v_ref, r_ref, gl_ref, wup_ref, bg_ref, ng_ref, tri_ref, o_ref,
                st_ref, *, ts):
    @pl.when(pl.program_id(1) == 0)
    def _init():
        st_ref[...] = jnp.zeros(st_ref.shape, F32)

    logit = jnp.dot(gl_ref[...], wup_ref[...], preferred_element_type=F32) + bg_ref[...]
    log_a = (jnp.minimum(logit, 0.0) - jnp.log1p(jnp.exp(-jnp.abs(logit)))) * (1.0 / GLA_TAU)
    tri = tri_ref[...]
    ri = lax.broadcasted_iota(I32, (GLA_CHUNK, GLA_CHUNK), 0)
    ci = lax.broadcasted_iota(I32, (GLA_CHUNK, GLA_CHUNK), 1)
    causal = ri >= ci
    nt = (((1,), (1,)), ((), ()))
    tn = (((0,), (0,)), ((), ()))
    for c in range(ts // GLA_CHUNK):
        rs = slice(c * GLA_CHUNK, (c + 1) * GLA_CHUNK)
        la = log_a[rs, :]
        bcum = jnp.dot(tri, la, precision=lax.Precision.HIGHEST, preferred_element_type=F32)
        bl = bcum[GLA_CHUNK - 1:GLA_CHUNK, :]
        kf = k_ref[rs, :].astype(F32)
        qd = (q_ref[rs, :].astype(F32) * (GLA_DK ** -0.5)) * jnp.exp(bcum)
        ki = kf * jnp.exp(-bcum)
        ke = kf * jnp.exp(bl - bcum)
        gl = jnp.exp(bl)
        for h in range(GLA_HEADS):
            hs = slice(h * GLA_DK, (h + 1) * GLA_DK)
            vs = slice(h * GLA_DV, (h + 1) * GLA_DV)
            qd_h = qd[:, hs].astype(BF16)
            sc = lax.dot_general(qd_h, ki[:, hs].astype(BF16), nt, preferred_element_type=F32)
            sc = jnp.where(causal, sc, 0.0)
            v_h = v_ref[rs, vs]
            st_t = st_ref[h]
            o = jnp.dot(sc.astype(BF16), v_h, preferred_element_type=F32)
            o = o + lax.dot_general(qd_h, st_t.astype(BF16), nt, preferred_element_type=F32)
            st_ref[h] = st_t * gl[:, hs] + lax.dot_general(
                v_h, ke[:, hs].astype(BF16), tn, preferred_element_type=F32)
            ms = jnp.mean(o * o, axis=-1, keepdims=True)
            o = o * lax.rsqrt(ms + RMS_EPS) * ng_ref[:, vs]
            o_ref[rs, vs] = (o * _silu(r_ref[rs, vs].astype(F32))).astype(BF16)


def _gla(p, wup, bg, ng, tri, batch, seq, ts):
    t = batch * seq
    nst = seq // ts
    row = lambda b, s: b * nst + s
    full = lambda shape: pl.BlockSpec(shape, lambda b, s: (0,) * len(shape))
    q_blk = (2 * D_LRU) // GLA_QK_DIM
    v_blk = (2 * D_LRU + 2 * GLA_QK_DIM) // GLA_V_DIM
    gl_blk = N_MAIN // GLOW_PAD
    return pl.pallas_call(
        functools.partial(_gla_kernel, ts=ts),
        grid=(batch, nst),
        in_specs=[pl.BlockSpec((ts, GLA_QK_DIM), lambda b, s: (row(b, s), q_blk)),
                  pl.BlockSpec((ts, GLA_QK_DIM), lambda b, s: (row(b, s), q_blk + 1)),
                  pl.BlockSpec((ts, GLA_V_DIM), lambda b, s: (row(b, s), v_blk)),
                  pl.BlockSpec((ts, GLA_V_DIM), lambda b, s: (row(b, s), v_blk + 1)),
                  pl.BlockSpec((ts, GLOW_PAD), lambda b, s: (row(b, s), gl_blk)),
                  full((GLOW_PAD, GLA_QK_DIM)), full((1, GLA_QK_DIM)), full((1, GLA_V_DIM)),
                  full((GLA_CHUNK, GLA_CHUNK))],
        out_specs=pl.BlockSpec((ts, GLA_V_DIM), lambda b, s: (row(b, s), 0)),
        out_shape=jax.ShapeDtypeStruct((t, GLA_V_DIM), BF16),
        scratch_shapes=[pltpu.VMEM((GLA_HEADS, GLA_DV, GLA_DK), F32)],
        compiler_params=_cparams(("arbitrary", "arbitrary"), 40),
        name="gla",
    )(p, p, p, p, p, wup, bg, ng, tri)


def _merge_kernel(lru_ref, gla_ref, ga_ref, gb_ref, x_ref, wa_ref, wb_ref, wo_ref, g_ref, b_ref,
                  h_ref):
    ya = jnp.dot(lru_ref[...], wa_ref[...], preferred_element_type=F32)
    yb = jnp.dot(gla_ref[...], wb_ref[...], preferred_element_type=F32)
    merged = _sigmoid(ga_ref[...].astype(F32)) * ya + _sigmoid(gb_ref[...].astype(F32)) * yb
    mix = jnp.dot(merged.astype(BF16), wo_ref[...], preferred_element_type=F32)
    h_ref[...] = _layer_norm(DN_ALPHA * x_ref[...] + mix, g_ref[...], b_ref[...])


def _merge(lru_o, gla_o, p, x2, wa, wb, wo, g, b, tm):
    t, d = x2.shape
    ga_blk = (2 * D_LRU + 2 * GLA_QK_DIM + 2 * GLA_V_DIM) // D_MODEL
    rowb = lambda c: pl.BlockSpec((tm, d), lambda i: (i, c))
    full = lambda shape: pl.BlockSpec(shape, lambda i: (0,) * len(shape),
                                      pipeline_mode=pl.Buffered(1))
    return pl.pallas_call(
        _merge_kernel,
        grid=(t // tm,),
        in_specs=[rowb(0), rowb(0), rowb(ga_blk), rowb(ga_blk + 1), rowb(0),
                  full((d, d)), full((d, d)), full((d, d)), full((1, d)), full((1, d))],
        out_specs=rowb(0),
        out_shape=jax.ShapeDtypeStruct((t, d), F32),
        compiler_params=_cparams(("arbitrary",), 48),
        name="merge",
    )(lru_o, gla_o, p, p, x2, wa, wb, wo, g, b)


def _route_kernel(h_ref, wrt_ref, bias_ref, upper_ref, idx_ref, w_ref, rank_ref, cnt_ref,
                  carry, *, tr):
    @pl.when(pl.program_id(0) == 0)
    def _init():
        carry[...] = jnp.zeros(carry.shape, F32)

    neg = -jnp.inf
    nt = (((1,), (1,)), ((), ()))
    logits = lax.dot_general(wrt_ref[...], h_ref[...], nt, precision=lax.Precision.HIGHEST,
                             preferred_element_type=F32)
    scores = _sigmoid(logits)
    biased = scores + bias_ref[...]

    gs = []
    for g in range(N_GROUPS):
        blk = biased[g * GROUP_SIZE:(g + 1) * GROUP_SIZE, :]
        m1 = jnp.max(blk, axis=0, keepdims=True)
        eq = blk == m1
        n1 = jnp.sum(eq.astype(F32), axis=0, keepdims=True)
        m2 = jnp.max(jnp.where(eq, neg, blk), axis=0, keepdims=True)
        gs.append(m1 + jnp.where(n1 >= 2.0, m1, m2))
    sel = [jnp.zeros((1, tr), jnp.bool_) for _ in range(N_GROUPS)]
    for _ in range(TOPK_GROUPS):
        cur = [jnp.where(sel[g], neg, gs[g]) for g in range(N_GROUPS)]
        m = cur[0]
        for g in range(1, N_GROUPS):
            m = jnp.maximum(m, cur[g])
        found = jnp.zeros((1, tr), jnp.bool_)
        for g in range(N_GROUPS):
            pick = jnp.logical_and(cur[g] == m, jnp.logical_not(found))
            sel[g] = jnp.logical_or(sel[g], pick)
            found = jnp.logical_or(found, pick)
    cur = jnp.concatenate(
        [jnp.where(sel[g], biased[g * GROUP_SIZE:(g + 1) * GROUP_SIZE, :], neg)
         for g in range(N_GROUPS)], axis=0)

    rowid = lax.broadcasted_iota(I32, (N_EXPERTS, tr), 0)
    picked = jnp.zeros((N_EXPERTS, tr), jnp.bool_)
    idxs, ws = [], []
    for _ in range(TOP_K):
        m = jnp.max(cur, axis=0, keepdims=True)
        first = jnp.min(jnp.where(cur == m, rowid, N_EXPERTS), axis=0, keepdims=True)
        onehot = rowid == first
        ws.append(jnp.sum(jnp.where(onehot, scores, 0.0), axis=0, keepdims=True))
        idxs.append(first)
        cur = jnp.where(onehot, neg, cur)
        picked = jnp.logical_or(picked, onehot)
    wsum = ws[0]
    for k in range(1, TOP_K):
        wsum = wsum + ws[k]

    pf = picked.astype(F32)
    before = jnp.dot(pf.astype(BF16), upper_ref[...], preferred_element_type=F32) + carry[...]
    for k in range(TOP_K):
        idx_ref[k:k + 1, :] = idxs[k]
        w_ref[k:k + 1, :] = ws[k] / wsum * ROUTED_SCALE
        rank_ref[k:k + 1, :] = jnp.sum(jnp.where(rowid == idxs[k], before, 0.0), axis=0,
                                       keepdims=True).astype(I32)
    carry[...] = carry[...] + jnp.sum(pf, axis=1, keepdims=True)
    cnt_ref[...] = carry[...]


def _route(h, wrt, bias_col, upper, tr):
    t, d = h.shape
    full = lambda shape: pl.BlockSpec(shape, lambda i: (0,) * len(shape))
    tok = lambda: pl.BlockSpec((TOP_K, tr), lambda i: (0, i))
    return pl.pallas_call(
        functools.partial(_route_kernel, tr=tr),
        grid=(t // tr,),
        in_specs=[pl.BlockSpec((tr, d), lambda i: (i, 0)),
                  full((N_EXPERTS, d)), full((N_EXPERTS, 1)), full((tr, tr))],
        out_specs=[tok(), tok(), tok(), full((N_EXPERTS, 1))],
        out_shape=[jax.ShapeDtypeStruct((TOP_K, t), I32), jax.ShapeDtypeStruct((TOP_K, t), F32),
                   jax.ShapeDtypeStruct((TOP_K, t), I32),
                   jax.ShapeDtypeStruct((N_EXPERTS, 1), F32)],
        scratch_shapes=[pltpu.VMEM((N_EXPERTS, 1), F32)],
        compiler_params=_cparams(("arbitrary",), 32),
        name="route",
    )(h, wrt, bias_col, upper)


def _dispatch_kernel(dest_ref, h_ref, xs_ref, sem, *, tm):
    def _issue(t, c):
        for k in range(TOP_K):
            pltpu.make_async_copy(h_ref.at[pl.ds(t, 1), :],
                                  xs_ref.at[pl.ds(dest_ref[k, t], 1), :], sem).start()
        return c

    lax.fori_loop(0, tm, _issue, 0)
    for _ in range(TOP_K):
        pltpu.make_async_copy(h_ref, xs_ref.at[pl.ds(0, tm), :], sem).wait()


def _dispatch(dest, h, n_slots, tm):
    t, d = h.shape
    return pl.pallas_call(
        functools.partial(_dispatch_kernel, tm=tm),
        grid=(t // tm,),
        in_specs=[pl.BlockSpec((TOP_K, tm), lambda i: (0, i), memory_space=pltpu.SMEM),
                  pl.BlockSpec((tm, d), lambda i: (i, 0))],
        out_specs=pl.BlockSpec(memory_space=pl.ANY),
        out_shape=jax.ShapeDtypeStruct((n_slots, d), F32),
        scratch_shapes=[pltpu.SemaphoreType.DMA(())],
        compiler_params=_cparams(("arbitrary",), 32),
        name="dispatch",
    )(dest, h)


def _expert_kernel(bmap_ref, bexp_ref, flag_ref, xs_ref, wg_ref, wu_ref, wd_ref, y_ref,
                   wgu_s, wd_s):
    i = pl.program_id(0)
    flags = flag_ref[i]

    @pl.when(flags >= 2)
    def _cast():
        wgu_s[:, :D_EXPERT] = wg_ref[...].astype(BF16)
        wgu_s[:, D_EXPERT:] = wu_ref[...].astype(BF16)
        wd_s[...] = wd_ref[...].astype(BF16)

    @pl.when(flags % 2 == 1)
    def _compute():
        z = jnp.dot(xs_ref[...].astype(BF16), wgu_s[...], preferred_element_type=F32)
        act = _silu(z[:, :D_EXPERT]) * z[:, D_EXPERT:]
        y_ref[...] = jnp.dot(act.astype(BF16), wd_s[...], preferred_element_type=F32)


def _experts(bmap, bexp, flags, xs, wg, wu, wd):
    n_slots, d = xs.shape
    nb = n_slots // EXPERT_ROWS
    grid_spec = pltpu.PrefetchScalarGridSpec(
        num_scalar_prefetch=3,
        grid=(nb,),
        in_specs=[pl.BlockSpec((EXPERT_ROWS, d), lambda i, bm, be, fl: (bm[i], 0)),
                  pl.BlockSpec((None, d, D_EXPERT), lambda i, bm, be, fl: (be[i], 0, 0)),
                  pl.BlockSpec((None, d, D_EXPERT), lambda i, bm, be, fl: (be[i], 0, 0)),
                  pl.BlockSpec((None, D_EXPERT, d), lambda i, bm, be, fl: (be[i], 0, 0))],
        out_specs=pl.BlockSpec((EXPERT_ROWS, d), lambda i, bm, be, fl: (bm[i], 0)),
        scratch_shapes=[pltpu.VMEM((d, 2 * D_EXPERT), BF16), pltpu.VMEM((D_EXPERT, d), BF16)],
    )
    return pl.pallas_call(
        _expert_kernel,
        grid_spec=grid_spec,
        out_shape=jax.ShapeDtypeStruct((n_slots, d), F32),
        compiler_params=_cparams(("arbitrary",), 32),
        name="experts",
    )(bmap, bexp, flags, xs, wg, wu, wd)


def _combine_kernel(dest_ref, h_ref, w_ref, y_ref, wsgu_ref, wsd_ref, g_ref, b_ref, o_ref,
                    ybuf, sem, *, tm):
    def _issue(t, c):
        for k in range(TOP_K):
            pltpu.make_async_copy(y_ref.at[pl.ds(dest_ref[k, t], 1), :],
                                  ybuf.at[k, pl.ds(t, 1), :], sem).start()
        return c

    lax.fori_loop(0, tm, _issue, 0)
    h = h_ref[...]
    z = jnp.dot(h.astype(BF16), wsgu_ref[...], preferred_element_type=F32)
    act = _silu(z[:, :D_SHARED]) * z[:, D_SHARED:]
    ffn = jnp.dot(act.astype(BF16), wsd_ref[...], preferred_element_type=F32)
    for k in range(TOP_K):
        pltpu.make_async_copy(y_ref.at[pl.ds(0, tm), :], ybuf.at[k], sem).wait()
    for k in range(TOP_K):
        ffn = ffn + ybuf[k] * w_ref[:, k:k + 1]
    o_ref[...] = _layer_norm(DN_ALPHA * h + ffn, g_ref[...], b_ref[...])


def _combine(dest, h, w_t, y, wsgu, wsd, g, b, tm):
    t, d = h.shape
    full = lambda shape: pl.BlockSpec(shape, lambda i: (0,) * len(shape))
    return pl.pallas_call(
        functools.partial(_combine_kernel, tm=tm),
        grid=(t // tm,),
        in_specs=[pl.BlockSpec((TOP_K, tm), lambda i: (0, i), memory_space=pltpu.SMEM),
                  pl.BlockSpec((tm, d), lambda i: (i, 0)),
                  pl.BlockSpec((tm, TOP_K), lambda i: (i, 0)),
                  pl.BlockSpec(memory_space=pl.ANY),
                  full((d, 2 * D_SHARED)), full((D_SHARED, d)), full((1, d)), full((1, d))],
        out_specs=pl.BlockSpec((tm, d), lambda i: (i, 0)),
        out_shape=jax.ShapeDtypeStruct((t, d), F32),
        scratch_shapes=[pltpu.VMEM((TOP_K, tm, d), F32), pltpu.SemaphoreType.DMA(())],
        compiler_params=_cparams(("arbitrary",), 40),
        name="combine",
    )(dest, h, w_t, y, wsgu, wsd, g, b)


def _block_diag4(w):
    n = w.shape[0] // 4
    w4 = w.reshape(n, 4, LRU_BLOCK_DIM, LRU_BLOCK_DIM)
    eye = jnp.eye(4, dtype=w.dtype)
    return jnp.einsum('gaij,ab->gaibj', w4, eye).reshape(n, 4 * LRU_BLOCK_DIM, 4 * LRU_BLOCK_DIM)


def _layer(x, w_in, conv_w, conv_b, lru_w_a, lru_b_a, lru_w_x, lru_b_x, lru_lambda, w_lru_out,
           gla_w_gate_up, gla_b_gate, gla_norm_g, w_gla_out, w_mix_out, ln1_g, ln1_b,
           w_router, router_bias, w_exp_gate, w_exp_up, w_exp_down, w_sh_gate, w_sh_up, w_sh_down,
           ln2_g, ln2_b):
    batch, seq, d = x.shape
    t = batch * seq
    x2 = x.reshape(t, d)
    row = lambda v: v.reshape(1, -1)

    glow_lo = 2 * D_LRU + 2 * GLA_QK_DIM + 2 * GLA_V_DIM
    glow_hi = glow_lo + GLA_GATE_RANK
    w_p = jnp.concatenate(
        [w_in[:, :glow_lo], w_in[:, glow_hi:], w_in[:, glow_lo:glow_hi],
         jnp.zeros((d, GLOW_PAD - GLA_GATE_RANK), w_in.dtype)], axis=1).astype(BF16)
    p = _proj(x2, w_p, tm=min(512, t))

    wg = jnp.concatenate([_block_diag4(lru_w_a), _block_diag4(lru_w_x)], axis=2).astype(BF16)
    ts_lru = min(256, seq)
    lru_o = _lru(p, conv_w, row(conv_b), wg, row(lru_b_a), row(lru_b_x), row(lru_lambda),
                 batch, seq, ts_lru)

    wup = jnp.concatenate(
        [gla_w_gate_up, jnp.zeros((GLOW_PAD - GLA_GATE_RANK, GLA_QK_DIM), gla_w_gate_up.dtype)],
        axis=0).astype(BF16)
    tri = jnp.tril(jnp.ones((GLA_CHUNK, GLA_CHUNK), F32))
    ts_gla = min(256, seq)
    gla_o = _gla(p, wup, row(gla_b_gate), row(gla_norm_g), tri, batch, seq, ts_gla)

    h = _merge(lru_o, gla_o, p, x2, w_lru_out.astype(BF16), w_gla_out.astype(BF16),
               w_mix_out.astype(BF16), row(ln1_g), row(ln1_b), tm=min(512, t))

    tr = min(256, t)
    upper = jnp.triu(jnp.ones((tr, tr), F32), k=1).astype(BF16)
    idx, w, rank, cnt = _route(h, w_router.T, router_bias.reshape(-1, 1), upper, tr)

    counts = cnt[:, 0].astype(I32)
    padded = (counts + EXPERT_ROWS - 1) // EXPERT_ROWS * EXPERT_ROWS
    pend = jnp.cumsum(padded)
    pstart = pend - padded
    dest = pstart[idx] + rank
    n_blocks = (t * TOP_K) // EXPERT_ROWS + N_EXPERTS
    n_slots = n_blocks * EXPERT_ROWS
    blk = jnp.arange(n_blocks, dtype=I32)
    n_used = pend[-1] // EXPERT_ROWS
    bmap = jnp.minimum(blk, n_used - 1)
    bexp = jnp.minimum(jnp.searchsorted(pend, bmap * EXPERT_ROWS, side='right'),
                       N_EXPERTS - 1).astype(I32)
    first = jnp.concatenate([jnp.ones((1,), I32), (bexp[1:] != bexp[:-1]).astype(I32)])
    flags = (blk < n_used).astype(I32) + 2 * first

    tm_d = min(512, t)
    xs = _dispatch(dest, h, n_slots, tm_d)
    y = _experts(bmap, bexp, flags, xs, w_exp_gate, w_exp_up, w_exp_down)

    wsgu = jnp.concatenate([w_sh_gate, w_sh_up], axis=1).astype(BF16)
    out = _combine(dest, h, w.T, y, wsgu, w_sh_down.astype(BF16), row(ln2_g), row(ln2_b),
                   tm=min(256, t))
    return out.reshape(batch, seq, d)


def kernel(x, w_in, conv_w, conv_b, lru_w_a, lru_b_a, lru_w_x, lru_b_x, lru_lambda, w_lru_out,
           gla_w_gate_up, gla_b_gate, gla_norm_g, w_gla_out, w_mix_out, ln1_g, ln1_b,
           w_router, router_bias, w_exp_gate, w_exp_up, w_exp_down, w_sh_gate, w_sh_up, w_sh_down,
           ln2_g, ln2_b):
    params = (w_in, conv_w, conv_b, lru_w_a, lru_b_a, lru_w_x, lru_b_x, lru_lambda, w_lru_out,
              gla_w_gate_up, gla_b_gate, gla_norm_g, w_gla_out, w_mix_out, ln1_g, ln1_b,
              w_router, router_bias, w_exp_gate, w_exp_up, w_exp_down, w_sh_gate, w_sh_up,
              w_sh_down, ln2_g, ln2_b)
    h = x
    for l in range(DEPTH):
        h = _layer(h, *(p[l] for p in params))
    return h
```

```python
import functools

import jax
import jax.numpy as jnp
from jax import lax
from jax.experimental import pallas as pl
from jax.experimental.pallas import tpu as pltpu
from jax.experimental.pallas import tpu_sc as plsc

F32 = jnp.float32
BF16 = jnp.bfloat16
I32 = jnp.int32
U32 = jnp.uint32

D_MODEL = 1024
D_LRU = D_MODEL
LRU_BLOCKS = 16
LRU_BLOCK_DIM = D_LRU // LRU_BLOCKS
CONV_WIDTH = 4
LRU_C = 8.0
GLA_HEADS = 4
GLA_QK_DIM = D_MODEL // 2
GLA_V_DIM = D_MODEL
GLA_DK = GLA_QK_DIM // GLA_HEADS
GLA_DV = GLA_V_DIM // GLA_HEADS
GLA_GATE_RANK = 16
GLA_TAU = 16.0
GLA_CHUNK = 64
N_EXPERTS = 256
TOP_K = 8
N_GROUPS = 8
GROUP_SIZE = N_EXPERTS // N_GROUPS
TOPK_GROUPS = 4
D_EXPERT = 256
D_SHARED = 256
ROUTED_SCALE = 2.5
DEPTH = 1
DN_ALPHA = (2.0 * DEPTH) ** 0.25
LN_EPS = 1e-5
RMS_EPS = 1e-6

LANES = 128
SUBLANES = 8
GLOW_PAD = LANES
N_MAIN = 2 * D_LRU + 2 * GLA_QK_DIM + 2 * GLA_V_DIM + 2 * D_MODEL
N_PROJ = N_MAIN + GLOW_PAD
PROJ_CHUNK = 512

EXPERT_ROWS = 512
SC_WINDOW = 128


def _cparams(semantics, vmem_mib):
    return pltpu.CompilerParams(dimension_semantics=semantics,
                                vmem_limit_bytes=vmem_mib * 1024 * 1024)


def _sigmoid(x):
    return jax.nn.sigmoid(x)


def _silu(x):
    return x * jax.nn.sigmoid(x)


def _gelu_tanh(x):
    c = 0.7978845608028654
    return x * (0.5 * (1.0 + jnp.tanh(c * (x + 0.044715 * (x * x * x)))))


def _pack_rows(v):
    n = v.shape[1] // 2
    lo = lax.bitcast_convert_type(v[:, :n].astype(BF16).astype(F32), U32)
    hi = lax.bitcast_convert_type(v[:, n:].astype(BF16).astype(F32), U32)
    return (lo >> 16) | (hi & jnp.uint32(0xFFFF0000))


def _unpack_rows(p):
    lo = lax.bitcast_convert_type(p << 16, F32)
    hi = lax.bitcast_convert_type(p & jnp.uint32(0xFFFF0000), F32)
    return lo, hi


def _layer_norm(z, g, b):
    mu = jnp.mean(z, axis=-1, keepdims=True)
    zc = z - mu
    var = jnp.mean(zc * zc, axis=-1, keepdims=True)
    return zc * lax.rsqrt(var + LN_EPS) * g + b


def _proj_kernel(x_ref, w_ref, o_ref):
    xb = x_ref[...].astype(BF16)
    for j in range(0, N_MAIN, PROJ_CHUNK):
        o_ref[:, j:j + PROJ_CHUNK] = jnp.dot(
            xb, w_ref[:, j:j + PROJ_CHUNK], preferred_element_type=F32).astype(BF16)
    o_ref[:, N_MAIN:] = jnp.dot(xb, w_ref[:, N_MAIN:], preferred_element_type=F32).astype(BF16)


def _proj(x2, w_p, tm):
    t, d = x2.shape
    return pl.pallas_call(
        _proj_kernel,
        grid=(t // tm,),
        in_specs=[pl.BlockSpec((tm, d), lambda i: (i, 0)),
                  pl.BlockSpec((d, N_PROJ), lambda i: (0, 0), pipeline_mode=pl.Buffered(1))],
        out_specs=pl.BlockSpec((tm, N_PROJ), lambda i: (i, 0)),
        out_shape=jax.ShapeDtypeStruct((t, N_PROJ), BF16),
        compiler_params=_cparams(("arbitrary",), 48),
        name="proj",
    )(x2, w_p)


def _lru_kernel(xa_ref, ga_ref, cw_ref, cb_ref, wg_ref, ba_ref, bx_ref, lam_ref, o_ref,
                xbuf, a_s, u_s, cin_s, carry, *, ts):
    nb = ts // SUBLANES
    nslab = D_LRU // LANES
    grp = 4 * LRU_BLOCK_DIM

    @pl.when(pl.program_id(1) == 0)
    def _init():
        xbuf[0:SUBLANES, :] = jnp.zeros((SUBLANES, D_LRU), F32)
        carry[...] = jnp.zeros((nslab, LANES), F32)

    xbuf[SUBLANES:, :] = xa_ref[...].astype(F32)
    xc = cb_ref[...]
    for j in range(CONV_WIDTH):
        lo = SUBLANES - (CONV_WIDTH - 1) + j
        xc = xc + cw_ref[j:j + 1, :] * xbuf[lo:lo + ts, :]
    xbuf[0:SUBLANES, :] = xbuf[ts:ts + SUBLANES, :]

    lam = lam_ref[...]
    sp = jnp.maximum(-lam, 0.0) + jnp.log1p(jnp.exp(-jnp.abs(lam)))
    for g in range(D_LRU // grp):
        sl = slice(g * grp, (g + 1) * grp)
        xg = xc[:, sl]
        z = jnp.dot(xg.astype(BF16), wg_ref[g], preferred_element_type=F32)
        r = _sigmoid(z[:, :grp] + ba_ref[:, sl])
        i = _sigmoid(z[:, grp:] + bx_ref[:, sl])
        log_a = (-LRU_C * r) * sp[:, sl]
        a = jnp.exp(log_a)
        u = jnp.sqrt(-jnp.tanh(log_a) * (a * a + 1.0)) * (i * xg)
        for h in range(grp // LANES):
            s = g * (grp // LANES) + h
            a_s[s] = a[:, h * LANES:(h + 1) * LANES]
            u_s[s] = u[:, h * LANES:(h + 1) * LANES]

    for s in range(nslab):
        hh = u_s[s, pl.ds(0, nb, stride=SUBLANES), :]
        aa = a_s[s, pl.ds(0, nb, stride=SUBLANES), :]
        for r in range(1, SUBLANES):
            a_r = a_s[s, pl.ds(r, nb, stride=SUBLANES), :]
            hh = a_r * hh + u_s[s, pl.ds(r, nb, stride=SUBLANES), :]
            aa = a_r * aa
            u_s[s, pl.ds(r, nb, stride=SUBLANES), :] = hh
            a_s[s, pl.ds(r, nb, stride=SUBLANES), :] = aa

    def _carry_step(b, cs):
        last = b * SUBLANES + (SUBLANES - 1)
        out = []
        for s in range(nslab):
            cin_s[s, pl.ds(b, 1), :] = cs[s]
            out.append(a_s[s, pl.ds(last, 1), :] * cs[s] + u_s[s, pl.ds(last, 1), :])
        return tuple(out)

    c0 = tuple(carry[s:s + 1, :] for s in range(nslab))
    cs = lax.fori_loop(0, nb, _carry_step, c0)
    for s in range(nslab):
        carry[s:s + 1, :] = cs[s]

    for s in range(nslab):
        cin = cin_s[s]
        for r in range(SUBLANES):
            hloc = u_s[s, pl.ds(r, nb, stride=SUBLANES), :]
            acum = a_s[s, pl.ds(r, nb, stride=SUBLANES), :]
            u_s[s, pl.ds(r, nb, stride=SUBLANES), :] = hloc + acum * cin

    for s in range(nslab):
        cs_ = slice(s * LANES, (s + 1) * LANES)
        o_ref[:, cs_] = (u_s[s] * _gelu_tanh(ga_ref[:, cs_].astype(F32))).astype(BF16)


def _lru(p, cw, cb, wg, ba, bx, lam, batch, seq, ts):
    t = batch * seq
    nst = seq // ts
    nslab = D_LRU // LANES
    row = lambda b, s: b * nst + s
    full = lambda shape: pl.BlockSpec(shape, lambda b, s: (0,) * len(shape))
    return pl.pallas_call(
        functools.partial(_lru_kernel, ts=ts),
        grid=(batch, nst),
        in_specs=[pl.BlockSpec((ts, D_LRU), lambda b, s: (row(b, s), 0)),
                  pl.BlockSpec((ts, D_LRU), lambda b, s: (row(b, s), 1)),
                  full((CONV_WIDTH, D_LRU)), full((1, D_LRU)),
                  full(wg.shape), full((1, D_LRU)), full((1, D_LRU)), full((1, D_LRU))],
        out_specs=pl.BlockSpec((ts, D_LRU), lambda b, s: (row(b, s), 0)),
        out_shape=jax.ShapeDtypeStruct((t, D_LRU), BF16),
        scratch_shapes=[pltpu.VMEM((ts + SUBLANES, D_LRU), F32),
                        pltpu.VMEM((nslab, ts, LANES), F32),
                        pltpu.VMEM((nslab, ts, LANES), F32),
                        pltpu.VMEM((nslab, ts // SUBLANES, LANES), F32),
                        pltpu.VMEM((nslab, LANES), F32)],
        compiler_params=_cparams(("arbitrary", "arbitrary"), 40),
        name="lru",
    )(p, p, cw, cb, wg, ba, bx, lam)


def _gla_kernel(q_ref, k_ref, v_ref, r_ref, gl_ref, wup_ref, bg_ref, ng_ref, tri_ref, o_ref,
                st_ref, *, ts):
    @pl.when(pl.program_id(1) == 0)
    def _init():
        st_ref[...] = jnp.zeros(st_ref.shape, F32)

    logit = jnp.dot(gl_ref[...], wup_ref[...], preferred_element_type=F32) + bg_ref[...]
    log_a = (jnp.minimum(logit, 0.0) - jnp.log1p(jnp.exp(-jnp.abs(logit)))) * (1.0 / GLA_TAU)
    tri = tri_ref[...]
    ri = lax.broadcasted_iota(I32, (GLA_CHUNK, GLA_CHUNK), 0)
    ci = lax.broadcasted_iota(I32, (GLA_CHUNK, GLA_CHUNK), 1)
    causal = ri >= ci
    nt = (((1,), (1,)), ((), ()))
    tn = (((0,), (0,)), ((), ()))
    for c in range(ts // GLA_CHUNK):
        rs = slice(c * GLA_CHUNK, (c + 1) * GLA_CHUNK)
        la = log_a[rs, :]
        bcum = jnp.dot(tri, la, precision=lax.Precision.HIGHEST, preferred_element_type=F32)
        bl = bcum[GLA_CHUNK - 1:GLA_CHUNK, :]
        kf = k_ref[rs, :].astype(F32)
        qd = (q_ref[rs, :].astype(F32) * (GLA_DK ** -0.5)) * jnp.exp(bcum)
        ki = kf * jnp.exp(-bcum)
        ke = kf * jnp.exp(bl - bcum)
        gl = jnp.exp(bl)
        for h in range(GLA_HEADS):
            hs = slice(h * GLA_DK, (h + 1) * GLA_DK)
            vs = slice(h * GLA_DV, (h + 1) * GLA_DV)
            qd_h = qd[:, hs].astype(BF16)
            sc = lax.dot_general(qd_h, ki[:, hs].astype(BF16), nt, preferred_element_type=F32)
            sc = jnp.where(causal, sc, 0.0)
            v_h = v_ref[rs, vs]
            st_t = st_ref[h]
            o = jnp.dot(sc.astype(BF16), v_h, preferred_element_type=F32)
            o = o + lax.dot_general(qd_h, st_t.astype(BF16), nt, preferred_element_type=F32)
            st_ref[h] = st_t * gl[:, hs] + lax.dot_general(
                v_h, ke[:, hs].astype(BF16), tn, preferred_element_type=F32)
            ms = jnp.mean(o * o, axis=-1, keepdims=True)
            o = o * lax.rsqrt(ms + RMS_EPS) * ng_ref[:, vs]
            o_ref[rs, vs] = (o * _silu(r_ref[rs, vs].astype(F32))).astype(BF16)


def _gla(p, wup, bg, ng, tri, batch, seq, ts):
    t = batch * seq
    nst = seq // ts
    row = lambda b, s: b * nst + s
    full = lambda shape: pl.BlockSpec(shape, lambda b, s: (0,) * len(shape))
    q_blk = (2 * D_LRU) // GLA_QK_DIM
    v_blk = (2 * D_LRU + 2 * GLA_QK_DIM) // GLA_V_DIM
    gl_blk = N_MAIN // GLOW_PAD
    return pl.pallas_call(
        functools.partial(_gla_kernel, ts=ts),
        grid=(batch, nst),
        in_specs=[pl.BlockSpec((ts, GLA_QK_DIM), lambda b, s: (row(b, s), q_blk)),
                  pl.BlockSpec((ts, GLA_QK_DIM), lambda b, s: (row(b, s), q_blk + 1)),
                  pl.BlockSpec((ts, GLA_V_DIM), lambda b, s: (row(b, s), v_blk)),
                  pl.BlockSpec((ts, GLA_V_DIM), lambda b, s: (row(b, s), v_blk + 1)),
                  pl.BlockSpec((ts, GLOW_PAD), lambda b, s: (row(b, s), gl_blk)),
                  full((GLOW_PAD, GLA_QK_DIM)), full((1, GLA_QK_DIM)), full((1, GLA_V_DIM)),
                  full((GLA_CHUNK, GLA_CHUNK))],
        out_specs=pl.BlockSpec((ts, GLA_V_DIM), lambda b, s: (row(b, s), 0)),
        out_shape=jax.ShapeDtypeStruct((t, GLA_V_DIM), BF16),
        scratch_shapes=[pltpu.VMEM((GLA_HEADS, GLA_DV, GLA_DK), F32)],
        compiler_params=_cparams(("arbitrary", "arbitrary"), 40),
        name="gla",
    )(p, p, p, p, p, wup, bg, ng, tri)


def _merge_kernel(lru_ref, gla_ref, ga_ref, gb_ref, x_ref, wa_ref, wb_ref, wo_ref, g_ref, b_ref,
                  h_ref, hp_ref):
    ya = jnp.dot(lru_ref[...], wa_ref[...], preferred_element_type=F32)
    yb = jnp.dot(gla_ref[...], wb_ref[...], preferred_element_type=F32)
    merged = _sigmoid(ga_ref[...].astype(F32)) * ya + _sigmoid(gb_ref[...].astype(F32)) * yb
    mix = jnp.dot(merged.astype(BF16), wo_ref[...], preferred_element_type=F32)
    h = _layer_norm(DN_ALPHA * x_ref[...] + mix, g_ref[...], b_ref[...])
    h_ref[...] = h
    hp_ref[...] = _pack_rows(h)


def _merge(lru_o, gla_o, p, x2, wa, wb, wo, g, b, tm):
    t, d = x2.shape
    ga_blk = (2 * D_LRU + 2 * GLA_QK_DIM + 2 * GLA_V_DIM) // D_MODEL
    rowb = lambda c: pl.BlockSpec((tm, d), lambda i: (i, c))
    full = lambda shape: pl.BlockSpec(shape, lambda i: (0,) * len(shape),
                                      pipeline_mode=pl.Buffered(1))
    return pl.pallas_call(
        _merge_kernel,
        grid=(t // tm,),
        in_specs=[rowb(0), rowb(0), rowb(ga_blk), rowb(ga_blk + 1), rowb(0),
                  full((d, d)), full((d, d)), full((d, d)), full((1, d)), full((1, d))],
        out_specs=[rowb(0), pl.BlockSpec((tm, d // 2), lambda i: (i, 0))],
        out_shape=[jax.ShapeDtypeStruct((t, d), F32), jax.ShapeDtypeStruct((t, d // 2), U32)],
        compiler_params=_cparams(("arbitrary",), 48),
        name="merge",
    )(lru_o, gla_o, p, p, x2, wa, wb, wo, g, b)


def _route_kernel(h_ref, wrt_ref, bias_ref, upper_ref, idx_ref, w_ref, rank_ref, cnt_ref,
                  carry, *, tr):
    @pl.when(pl.program_id(0) == 0)
    def _init():
        carry[...] = jnp.zeros(carry.shape, F32)

    neg = -jnp.inf
    nt = (((1,), (1,)), ((), ()))
    logits = lax.dot_general(wrt_ref[...], h_ref[...], nt, precision=lax.Precision.HIGHEST,
                             preferred_element_type=F32)
    scores = _sigmoid(logits)
    biased = scores + bias_ref[...]

    gs = []
    for g in range(N_GROUPS):
        blk = biased[g * GROUP_SIZE:(g + 1) * GROUP_SIZE, :]
        m1 = jnp.max(blk, axis=0, keepdims=True)
        eq = blk == m1
        n1 = jnp.sum(eq.astype(F32), axis=0, keepdims=True)
        m2 = jnp.max(jnp.where(eq, neg, blk), axis=0, keepdims=True)
        gs.append(m1 + jnp.where(n1 >= 2.0, m1, m2))
    sel = [jnp.zeros((1, tr), jnp.bool_) for _ in range(N_GROUPS)]
    for _ in range(TOPK_GROUPS):
        cur = [jnp.where(sel[g], neg, gs[g]) for g in range(N_GROUPS)]
        m = cur[0]
        for g in range(1, N_GROUPS):
            m = jnp.maximum(m, cur[g])
        found = jnp.zeros((1, tr), jnp.bool_)
        for g in range(N_GROUPS):
            pick = jnp.logical_and(cur[g] == m, jnp.logical_not(found))
            sel[g] = jnp.logical_or(sel[g], pick)
            found = jnp.logical_or(found, pick)
    cur = jnp.concatenate(
        [jnp.where(sel[g], biased[g * GROUP_SIZE:(g + 1) * GROUP_SIZE, :], neg)
         for g in range(N_GROUPS)], axis=0)

    rowid = lax.broadcasted_iota(I32, (N_EXPERTS, tr), 0)
    picked = jnp.zeros((N_EXPERTS, tr), jnp.bool_)
    idxs, ws = [], []
    for _ in range(TOP_K):
        m = jnp.max(cur, axis=0, keepdims=True)
        first = jnp.min(jnp.where(cur == m, rowid, N_EXPERTS), axis=0, keepdims=True)
        onehot = rowid == first
        ws.append(jnp.sum(jnp.where(onehot, scores, 0.0), axis=0, keepdims=True))
        idxs.append(first)
        cur = jnp.where(onehot, neg, cur)
        picked = jnp.logical_or(picked, onehot)
    wsum = ws[0]
    for k in range(1, TOP_K):
        wsum = wsum + ws[k]

    pf = picked.astype(F32)
    before = jnp.dot(pf.astype(BF16), upper_ref[...], preferred_element_type=F32) + carry[...]
    for k in range(TOP_K):
        idx_ref[k:k + 1, :] = idxs[k]
        w_ref[k:k + 1, :] = ws[k] / wsum * ROUTED_SCALE
        rank_ref[k:k + 1, :] = jnp.sum(jnp.where(rowid == idxs[k], before, 0.0), axis=0,
                                       keepdims=True).astype(I32)
    carry[...] = carry[...] + jnp.sum(pf, axis=1, keepdims=True)
    cnt_ref[...] = carry[...]


def _route(h, wrt, bias_col, upper, tr):
    t, d = h.shape
    full = lambda shape: pl.BlockSpec(shape, lambda i: (0,) * len(shape))
    tok = lambda: pl.BlockSpec((TOP_K, tr), lambda i: (0, i))
    return pl.pallas_call(
        functools.partial(_route_kernel, tr=tr),
        grid=(t // tr,),
        in_specs=[pl.BlockSpec((tr, d), lambda i: (i, 0)),
                  full((N_EXPERTS, d)), full((N_EXPERTS, 1)), full((tr, tr))],
        out_specs=[tok(), tok(), tok(), full((N_EXPERTS, 1))],
        out_shape=[jax.ShapeDtypeStruct((TOP_K, t), I32), jax.ShapeDtypeStruct((TOP_K, t), F32),
                   jax.ShapeDtypeStruct((TOP_K, t), I32),
                   jax.ShapeDtypeStruct((N_EXPERTS, 1), F32)],
        scratch_shapes=[pltpu.VMEM((N_EXPERTS, 1), F32)],
        compiler_params=_cparams(("arbitrary",), 32),
        name="route",
    )(h, wrt, bias_col, upper)


def _slots_kernel(idx_ref, rank_ref, pstart_ref, dest_ref, *, tl):
    rowid = lax.broadcasted_iota(I32, (N_EXPERTS, tl), 0)
    pstart = pstart_ref[...]
    for k in range(TOP_K):
        base = jnp.sum(jnp.where(rowid == idx_ref[k:k + 1, :], pstart, 0.0), axis=0, keepdims=True)
        dest_ref[k:k + 1, :] = base.astype(I32) + rank_ref[k:k + 1, :]


def _slots(idx, rank, pstart_col, tl):
    t = idx.shape[1]
    tok = lambda: pl.BlockSpec((TOP_K, tl), lambda i: (0, i))
    return pl.pallas_call(
        functools.partial(_slots_kernel, tl=tl),
        grid=(t // tl,),
        in_specs=[tok(), tok(), pl.BlockSpec((N_EXPERTS, 1), lambda i: (0, 0))],
        out_specs=tok(),
        out_shape=jax.ShapeDtypeStruct((TOP_K, t), I32),
        compiler_params=_cparams(("arbitrary",), 32),
        name="slots",
    )(idx, rank, pstart_col)


def _dispatch_kernel(dest_ref, zblk_ref, hp_ref, xs_ref, zero_s, sem, zsem, *, tm):
    @pl.when(pl.program_id(0) == 0)
    def _clear():
        zero_s[...] = jnp.zeros(zero_s.shape, zero_s.dtype)
        nz = zblk_ref.shape[0]

        def _zero(j, c):
            @pl.when(zblk_ref[j] >= 0)
            def _():
                start = pl.multiple_of(zblk_ref[j] * EXPERT_ROWS, EXPERT_ROWS)
                pltpu.make_async_copy(zero_s, xs_ref.at[pl.ds(start, EXPERT_ROWS), :],
                                      zsem).start()
            return c

        def _zwait(j, c):
            @pl.when(zblk_ref[j] >= 0)
            def _():
                pltpu.make_async_copy(zero_s, xs_ref.at[pl.ds(0, EXPERT_ROWS), :], zsem).wait()
            return c

        lax.fori_loop(0, nz, _zero, 0)
        lax.fori_loop(0, nz, _zwait, 0)

    def _issue(t, c):
        for k in range(TOP_K):
            pltpu.make_async_copy(hp_ref.at[pl.ds(t, 1), :],
                                  xs_ref.at[pl.ds(dest_ref[k, t], 1), :], sem).start()
        return c

    lax.fori_loop(0, tm, _issue, 0)
    for _ in range(TOP_K):
        pltpu.make_async_copy(hp_ref, xs_ref.at[pl.ds(0, tm), :], sem).wait()


def _dispatch(dest, zblk, hp, n_slots, tm):
    t, dp = hp.shape
    return pl.pallas_call(
        functools.partial(_dispatch_kernel, tm=tm),
        grid=(t // tm,),
        in_specs=[pl.BlockSpec((TOP_K, tm), lambda i: (0, i), memory_space=pltpu.SMEM),
                  pl.BlockSpec(memory_space=pltpu.SMEM),
                  pl.BlockSpec((tm, dp), lambda i: (i, 0))],
        out_specs=pl.BlockSpec(memory_space=pl.ANY),
        out_shape=jax.ShapeDtypeStruct((n_slots, dp), hp.dtype),
        scratch_shapes=[pltpu.VMEM((EXPERT_ROWS, dp), hp.dtype),
                        pltpu.SemaphoreType.DMA(()), pltpu.SemaphoreType.DMA(())],
        compiler_params=_cparams(("arbitrary",), 32),
        name="dispatch",
    )(dest, zblk, hp)


def _expert_kernel(bmap_ref, bexp_ref, flag_ref, xs_ref, wg_ref, wu_ref, wd_ref, ya_ref, yb_ref,
                   wgu_s, wd_s):
    i = pl.program_id(0)
    flags = flag_ref[i]
    half = D_MODEL // 2

    @pl.when(flags >= 2)
    def _cast():
        wgu_s[:, :D_EXPERT] = wg_ref[...].astype(BF16)
        wgu_s[:, D_EXPERT:] = wu_ref[...].astype(BF16)
        wd_s[...] = wd_ref[...].astype(BF16)

    @pl.when(flags % 2 == 1)
    def _compute():
        lo, hi = _unpack_rows(xs_ref[...])
        z = (jnp.dot(lo.astype(BF16), wgu_s[:half, :], preferred_element_type=F32)
             + jnp.dot(hi.astype(BF16), wgu_s[half:, :], preferred_element_type=F32))
        act = _silu(z[:, :D_EXPERT]) * z[:, D_EXPERT:]
        yp = _pack_rows(jnp.dot(act.astype(BF16), wd_s[...], preferred_element_type=F32))
        ya_ref[...] = yp[:, :half // 2]
        yb_ref[...] = yp[:, half // 2:]

    @pl.when(flags % 2 == 0)
    def _tail():
        ya_ref[...] = jnp.zeros(ya_ref.shape, ya_ref.dtype)
        yb_ref[...] = jnp.zeros(yb_ref.shape, yb_ref.dtype)


def _experts(bmap, bexp, flags, xs, wg, wu, wd):
    n_slots, dp = xs.shape
    d = 2 * dp
    nb = n_slots // EXPERT_ROWS
    grid_spec = pltpu.PrefetchScalarGridSpec(
        num_scalar_prefetch=3,
        grid=(nb,),
        in_specs=[pl.BlockSpec((EXPERT_ROWS, dp), lambda i, bm, be, fl: (bm[i], 0)),
                  pl.BlockSpec((None, d, D_EXPERT), lambda i, bm, be, fl: (be[i], 0, 0)),
                  pl.BlockSpec((None, d, D_EXPERT), lambda i, bm, be, fl: (be[i], 0, 0)),
                  pl.BlockSpec((None, D_EXPERT, d), lambda i, bm, be, fl: (be[i], 0, 0))],
        out_specs=[pl.BlockSpec((EXPERT_ROWS, dp // 2), lambda i, bm, be, fl: (i, 0))] * 2,
        scratch_shapes=[pltpu.VMEM((d, 2 * D_EXPERT), BF16), pltpu.VMEM((D_EXPERT, d), BF16)],
    )
    return pl.pallas_call(
        _expert_kernel,
        grid_spec=grid_spec,
        out_shape=[jax.ShapeDtypeStruct((n_slots, dp // 2), U32)] * 2,
        compiler_params=_cparams(("arbitrary",), 32),
        name="experts",
    )(bmap, bexp, flags, xs, wg, wu, wd)


def _sc_gather_rows(table, idx):
    n = idx.shape[0]
    dp = table.shape[1]
    mesh = plsc.VectorSubcoreMesh(core_axis_name="core", subcore_axis_name="subcore")

    @pl.kernel(out_type=jax.ShapeDtypeStruct((n, dp), table.dtype), mesh=mesh, name="sc_gather")
    def _gather(table_hbm, idx_hbm, out_hbm):
        def _body(idx_vmem, out_vmem):
            pltpu.sync_copy(table_hbm.at[idx_vmem.at[0]], out_vmem)

        pltpu.emit_pipeline(
            _body,
            grid=(n // SC_WINDOW,),
            in_specs=[pl.BlockSpec((1, SC_WINDOW), lambda i: (0, i))],
            out_specs=[pl.BlockSpec((SC_WINDOW, dp), lambda i: (i, 0))],
            core_axis_name=("core", "subcore"),
            dimension_semantics=(pltpu.PARALLEL,),
        )(idx_hbm, out_hbm)

    return _gather(table, idx.reshape(1, n))


def _combine_kernel(h_ref, w_ref, *rest):
    ya_refs = rest[:TOP_K]
    yb_refs = rest[TOP_K:2 * TOP_K]
    wsgu_ref, wsd_ref, g_ref, b_ref, o_ref = rest[2 * TOP_K:]
    h = h_ref[...]
    z = jnp.dot(h.astype(BF16), wsgu_ref[...], preferred_element_type=F32)
    act = _silu(z[:, :D_SHARED]) * z[:, D_SHARED:]
    ffn = jnp.dot(act.astype(BF16), wsd_ref[...], preferred_element_type=F32)
    acc = [None] * 4
    for k in range(TOP_K):
        wk = w_ref[:, k:k + 1]
        parts = _unpack_rows(ya_refs[k][...]) + _unpack_rows(yb_refs[k][...])
        for j in range(4):
            acc[j] = parts[j] * wk if k == 0 else acc[j] + parts[j] * wk
    ffn = ffn + jnp.concatenate([acc[0], acc[2], acc[1], acc[3]], axis=1)
    o_ref[...] = _layer_norm(DN_ALPHA * h + ffn, g_ref[...], b_ref[...])


def _combine(h, w_t, yga, ygb, wsgu, wsd, g, b, tm):
    t, d = h.shape
    nt = t // tm
    full = lambda shape: pl.BlockSpec(shape, lambda i: (0,) * len(shape))
    y_spec = lambda k: pl.BlockSpec((tm, d // 4), lambda i: (k * nt + i, 0))
    return pl.pallas_call(
        _combine_kernel,
        grid=(nt,),
        in_specs=[pl.BlockSpec((tm, d), lambda i: (i, 0)),
                  pl.BlockSpec((tm, TOP_K), lambda i: (i, 0))]
                 + [y_spec(k) for k in range(TOP_K)] * 2
                 + [full((d, 2 * D_SHARED)), full((D_SHARED, d)), full((1, d)), full((1, d))],
        out_specs=pl.BlockSpec((tm, d), lambda i: (i, 0)),
        out_shape=jax.ShapeDtypeStruct((t, d), F32),
        compiler_params=_cparams(("arbitrary",), 40),
        name="combine",
    )(h, w_t, *([yga] * TOP_K), *([ygb] * TOP_K), wsgu, wsd, g, b)


def _block_diag4(w):
    n = w.shape[0] // 4
    w4 = w.reshape(n, 4, LRU_BLOCK_DIM, LRU_BLOCK_DIM)
    eye = jnp.eye(4, dtype=w.dtype)
    return jnp.einsum('gaij,ab->gaibj', w4, eye).reshape(n, 4 * LRU_BLOCK_DIM, 4 * LRU_BLOCK_DIM)


def _layer(x, w_in, conv_w, conv_b, lru_w_a, lru_b_a, lru_w_x, lru_b_x, lru_lambda, w_lru_out,
           gla_w_gate_up, gla_b_gate, gla_norm_g, w_gla_out, w_mix_out, ln1_g, ln1_b,
           w_router, router_bias, w_exp_gate, w_exp_up, w_exp_down, w_sh_gate, w_sh_up, w_sh_down,
           ln2_g, ln2_b):
    batch, seq, d = x.shape
    t = batch * seq
    x2 = x.reshape(t, d)
    row = lambda v: v.reshape(1, -1)

    glow_lo = 2 * D_LRU + 2 * GLA_QK_DIM + 2 * GLA_V_DIM
    glow_hi = glow_lo + GLA_GATE_RANK
    w_p = jnp.concatenate(
        [w_in[:, :glow_lo], w_in[:, glow_hi:], w_in[:, glow_lo:glow_hi],
         jnp.zeros((d, GLOW_PAD - GLA_GATE_RANK), w_in.dtype)], axis=1).astype(BF16)
    p = _proj(x2, w_p, tm=min(512, t))

    wg = jnp.concatenate([_block_diag4(lru_w_a), _block_diag4(lru_w_x)], axis=2).astype(BF16)
    ts_lru = min(256, seq)
    lru_o = _lru(p, conv_w, row(conv_b), wg, row(lru_b_a), row(lru_b_x), row(lru_lambda),
                 batch, seq, ts_lru)

    wup = jnp.concatenate(
        [gla_w_gate_up, jnp.zeros((GLOW_PAD - GLA_GATE_RANK, GLA_QK_DIM), gla_w_gate_up.dtype)],
        axis=0).astype(BF16)
    tri = jnp.tril(jnp.ones((GLA_CHUNK, GLA_CHUNK), F32))
    ts_gla = min(256, seq)
    gla_o = _gla(p, wup, row(gla_b_gate), row(gla_norm_g), tri, batch, seq, ts_gla)

    h, hp = _merge(lru_o, gla_o, p, x2, w_lru_out.astype(BF16), w_gla_out.astype(BF16),
                   w_mix_out.astype(BF16), row(ln1_g), row(ln1_b), tm=min(512, t))

    tr = min(256, t)
    upper = jnp.triu(jnp.ones((tr, tr), F32), k=1).astype(BF16)
    idx, w, rank, cnt = _route(h, w_router.T, router_bias.reshape(-1, 1), upper, tr)

    counts = cnt[:, 0].astype(I32)
    padded = (counts + EXPERT_ROWS - 1) // EXPERT_ROWS * EXPERT_ROWS
    pend = jnp.cumsum(padded)
    pstart = pend - padded
    dest = _slots(idx, rank, pstart.astype(F32).reshape(-1, 1), tl=min(512, t))
    n_blocks = (t * TOP_K) // EXPERT_ROWS + N_EXPERTS
    n_slots = n_blocks * EXPERT_ROWS
    blk = jnp.arange(n_blocks, dtype=I32)
    n_used = pend[-1] // EXPERT_ROWS
    bmap = jnp.minimum(blk, n_used - 1)
    bexp = jnp.minimum(
        jnp.sum((pend[None, :] <= (bmap * EXPERT_ROWS)[:, None]).astype(I32), axis=1),
        N_EXPERTS - 1)
    first = jnp.concatenate([jnp.ones((1,), I32), (bexp[1:] != bexp[:-1]).astype(I32)])
    flags = (blk < n_used).astype(I32) + 2 * first

    tail = n_used + jnp.arange(n_blocks - (t * TOP_K) // EXPERT_ROWS, dtype=I32)
    zblk = jnp.concatenate([
        jnp.where(padded > 0, pend // EXPERT_ROWS - 1, -1),
        jnp.where(tail < n_blocks, tail, -1)]).astype(I32)
    tm_d = min(512, t)
    xs = _dispatch(dest, zblk, hp, n_slots, tm_d)
    ya, yb = _experts(bmap, bexp, flags, xs, w_exp_gate, w_exp_up, w_exp_down)

    wsgu = jnp.concatenate([w_sh_gate, w_sh_up], axis=1).astype(BF16)
    dest_flat = dest.reshape(-1)
    out = _combine(h, w.T, _sc_gather_rows(ya, dest_flat), _sc_gather_rows(yb, dest_flat), wsgu,
                   w_sh_down.astype(BF16), row(ln2_g), row(ln2_b), tm=min(512, t))
    return out.reshape(batch, seq, d)


def kernel(x, w_in, conv_w, conv_b, lru_w_a, lru_b_a, lru_w_x, lru_b_x, lru_lambda, w_lru_out,
           gla_w_gate_up, gla_b_gate, gla_norm_g, w_gla_out, w_mix_out, ln1_g, ln1_b,
           w_router, router_bias, w_exp_gate, w_exp_up, w_exp_down, w_sh_gate, w_sh_up, w_sh_down,
           ln2_g, ln2_b):
    params = (w_in, conv_w, conv_b, lru_w_a, lru_b_a, lru_w_x, lru_b_x, lru_lambda, w_lru_out,
              gla_w_gate_up, gla_b_gate, gla_norm_g, w_gla_out, w_mix_out, ln1_g, ln1_b,
              w_router, router_bias, w_exp_gate, w_exp_up, w_exp_down, w_sh_gate, w_sh_up,
              w_sh_down, ln2_g, ln2_b)
    h = x
    for l in range(DEPTH):
        h = _layer(h, *(p[l] for p in params))
    return h
```

```python
import functools

import jax
import jax.numpy as jnp
from jax import lax
from jax.experimental import pallas as pl
from jax.experimental.pallas import tpu as pltpu
from jax.experimental.pallas import tpu_sc as plsc

F32 = jnp.float32
BF16 = jnp.bfloat16
I32 = jnp.int32
U32 = jnp.uint32

D_MODEL = 1024
D_LRU = D_MODEL
LRU_BLOCKS = 16
LRU_BLOCK_DIM = D_LRU // LRU_BLOCKS
CONV_WIDTH = 4
LRU_C = 8.0
GLA_HEADS = 4
GLA_QK_DIM = D_MODEL // 2
GLA_V_DIM = D_MODEL
GLA_DK = GLA_QK_DIM // GLA_HEADS
GLA_DV = GLA_V_DIM // GLA_HEADS
GLA_GATE_RANK = 16
GLA_TAU = 16.0
GLA_CHUNK = 64
N_EXPERTS = 256
TOP_K = 8
N_GROUPS = 8
GROUP_SIZE = N_EXPERTS // N_GROUPS
TOPK_GROUPS = 4
D_EXPERT = 256
D_SHARED = 256
ROUTED_SCALE = 2.5
DEPTH = 1
DN_ALPHA = (2.0 * DEPTH) ** 0.25
LN_EPS = 1e-5
RMS_EPS = 1e-6

LANES = 128
SUBLANES = 8
GLOW_PAD = LANES
N_MAIN = 2 * D_LRU + 2 * GLA_QK_DIM + 2 * GLA_V_DIM + 2 * D_MODEL
N_PROJ = N_MAIN + GLOW_PAD
PROJ_CHUNK = 512

EXPERT_ROWS = 512
SC_WINDOW = 128


def _cparams(semantics, vmem_mib):
    return pltpu.CompilerParams(dimension_semantics=semantics,
                                vmem_limit_bytes=vmem_mib * 1024 * 1024)


def _sigmoid(x):
    return jax.nn.sigmoid(x)


def _silu(x):
    return x * jax.nn.sigmoid(x)


def _gelu_tanh(x):
    c = 0.7978845608028654
    return x * (0.5 * (1.0 + jnp.tanh(c * (x + 0.044715 * (x * x * x)))))


def _pack_rows(v):
    n = v.shape[1] // 2
    lo = lax.bitcast_convert_type(v[:, :n].astype(BF16).astype(F32), U32)
    hi = lax.bitcast_convert_type(v[:, n:].astype(BF16).astype(F32), U32)
    return (lo >> 16) | (hi & jnp.uint32(0xFFFF0000))


def _unpack_rows(p):
    lo = lax.bitcast_convert_type(p << 16, F32)
    hi = lax.bitcast_convert_type(p & jnp.uint32(0xFFFF0000), F32)
    return lo, hi


def _layer_norm(z, g, b):
    mu = jnp.mean(z, axis=-1, keepdims=True)
    zc = z - mu
    var = jnp.mean(zc * zc, axis=-1, keepdims=True)
    return zc * lax.rsqrt(var + LN_EPS) * g + b


def _proj_kernel(x_ref, w_ref, o_ref):
    xb = x_ref[...].astype(BF16)
    for j in range(0, N_MAIN, PROJ_CHUNK):
        o_ref[:, j:j + PROJ_CHUNK] = jnp.dot(
            xb, w_ref[:, j:j + PROJ_CHUNK], preferred_element_type=F32).astype(BF16)
    o_ref[:, N_MAIN:] = jnp.dot(xb, w_ref[:, N_MAIN:], preferred_element_type=F32).astype(BF16)


def _proj(x2, w_p, tm):
    t, d = x2.shape
    return pl.pallas_call(
        _proj_kernel,
        grid=(t // tm,),
        in_specs=[pl.BlockSpec((tm, d), lambda i: (i, 0)),
                  pl.BlockSpec((d, N_PROJ), lambda i: (0, 0), pipeline_mode=pl.Buffered(1))],
        out_specs=pl.BlockSpec((tm, N_PROJ), lambda i: (i, 0)),
        out_shape=jax.ShapeDtypeStruct((t, N_PROJ), BF16),
        compiler_params=_cparams(("arbitrary",), 48),
        name="proj",
    )(x2, w_p)


def _lru_kernel(xa_ref, ga_ref, cw_ref, cb_ref, wg_ref, ba_ref, bx_ref, lam_ref, o_ref,
                xbuf, a_s, u_s, cin_s, carry, *, ts):
    nb = ts // SUBLANES
    nslab = D_LRU // LANES
    grp = 4 * LRU_BLOCK_DIM

    @pl.when(pl.program_id(1) == 0)
    def _init():
        xbuf[0:SUBLANES, :] = jnp.zeros((SUBLANES, D_LRU), F32)
        carry[...] = jnp.zeros((nslab, LANES), F32)

    xbuf[SUBLANES:, :] = xa_ref[...].astype(F32)
    xc = cb_ref[...]
    for j in range(CONV_WIDTH):
        lo = SUBLANES - (CONV_WIDTH - 1) + j
        xc = xc + cw_ref[j:j + 1, :] * xbuf[lo:lo + ts, :]
    xbuf[0:SUBLANES, :] = xbuf[ts:ts + SUBLANES, :]

    lam = lam_ref[...]
    sp = jnp.maximum(-lam, 0.0) + jnp.log1p(jnp.exp(-jnp.abs(lam)))
    for g in range(D_LRU // grp):
        sl = slice(g * grp, (g + 1) * grp)
        xg = xc[:, sl]
        z = jnp.dot(xg.astype(BF16), wg_ref[g], preferred_element_type=F32)
        r = _sigmoid(z[:, :grp] + ba_ref[:, sl])
        i = _sigmoid(z[:, grp:] + bx_ref[:, sl])
        log_a = (-LRU_C * r) * sp[:, sl]
        a = jnp.exp(log_a)
        u = jnp.sqrt(-jnp.tanh(log_a) * (a * a + 1.0)) * (i * xg)
        for h in range(grp // LANES):
            s = g * (grp // LANES) + h
            a_s[s] = a[:, h * LANES:(h + 1) * LANES]
            u_s[s] = u[:, h * LANES:(h + 1) * LANES]

    for s in range(nslab):
        hh = u_s[s, pl.ds(0, nb, stride=SUBLANES), :]
        aa = a_s[s, pl.ds(0, nb, stride=SUBLANES), :]
        for r in range(1, SUBLANES):
            a_r = a_s[s, pl.ds(r, nb, stride=SUBLANES), :]
            hh = a_r * hh + u_s[s, pl.ds(r, nb, stride=SUBLANES), :]
            aa = a_r * aa
            u_s[s, pl.ds(r, nb, stride=SUBLANES), :] = hh
            a_s[s, pl.ds(r, nb, stride=SUBLANES), :] = aa

    def _carry_step(b, cs):
        last = b * SUBLANES + (SUBLANES - 1)
        out = []
        for s in range(nslab):
            cin_s[s, pl.ds(b, 1), :] = cs[s]
            out.append(a_s[s, pl.ds(last, 1), :] * cs[s] + u_s[s, pl.ds(last, 1), :])
        return tuple(out)

    c0 = tuple(carry[s:s + 1, :] for s in range(nslab))
    cs = lax.fori_loop(0, nb, _carry_step, c0)
    for s in range(nslab):
        carry[s:s + 1, :] = cs[s]

    for s in range(nslab):
        cin = cin_s[s]
        for r in range(SUBLANES):
            hloc = u_s[s, pl.ds(r, nb, stride=SUBLANES), :]
            acum = a_s[s, pl.ds(r, nb, stride=SUBLANES), :]
            u_s[s, pl.ds(r, nb, stride=SUBLANES), :] = hloc + acum * cin

    for s in range(nslab):
        cs_ = slice(s * LANES, (s + 1) * LANES)
        o_ref[:, cs_] = (u_s[s] * _gelu_tanh(ga_ref[:, cs_].astype(F32))).astype(BF16)


def _lru(p, cw, cb, wg, ba, bx, lam, batch, seq, ts):
    t = batch * seq
    nst = seq // ts
    nslab = D_LRU // LANES
    row = lambda b, s: b * nst + s
    full = lambda shape: pl.BlockSpec(shape, lambda b, s: (0,) * len(shape))
    return pl.pallas_call(
        functools.partial(_lru_kernel, ts=ts),
        grid=(batch, nst),
        in_specs=[pl.BlockSpec((ts, D_LRU), lambda b, s: (row(b, s), 0)),
                  pl.BlockSpec((ts, D_LRU), lambda b, s: (row(b, s), 1)),
                  full((CONV_WIDTH, D_LRU)), full((1, D_LRU)),
                  full(wg.shape), full((1, D_LRU)), full((1, D_LRU)), full((1, D_LRU))],
        out_specs=pl.BlockSpec((ts, D_LRU), lambda b, s: (row(b, s), 0)),
        out_shape=jax.ShapeDtypeStruct((t, D_LRU), BF16),
        scratch_shapes=[pltpu.VMEM((ts + SUBLANES, D_LRU), F32),
                        pltpu.VMEM((nslab, ts, LANES), F32),
                        pltpu.VMEM((nslab, ts, LANES), F32),
                        pltpu.VMEM((nslab, ts // SUBLANES, LANES), F32),
                        pltpu.VMEM((nslab, LANES), F32)],
        compiler_params=_cparams(("arbitrary", "arbitrary"), 40),
        name="lru",
    )(p, p, cw, cb, wg, ba, bx, lam)


def _gla_kernel(q_ref, k_ref, v_ref, r_ref, gl_ref, wup_ref, bg_ref, ng_ref, tri_ref, o_ref,
                st_ref, *, ts):
    @pl.when(pl.program_id(1) == 0)
    def _init():
        st_ref[...] = jnp.zeros(st_ref.shape, F32)

    logit = jnp.dot(gl_ref[...], wup_ref[...], preferred_element_type=F32) + bg_ref[...]
    log_a = (jnp.minimum(logit, 0.0) - jnp.log1p(jnp.exp(-jnp.abs(logit)))) * (1.0 / GLA_TAU)
    tri = tri_ref[...]
    ri = lax.broadcasted_iota(I32, (GLA_CHUNK, GLA_CHUNK), 0)
    ci = lax.broadcasted_iota(I32, (GLA_CHUNK, GLA_CHUNK), 1)
    causal = ri >= ci
    nt = (((1,), (1,)), ((), ()))
    tn = (((0,), (0,)), ((), ()))
    for c in range(ts // GLA_CHUNK):
        rs = slice(c * GLA_CHUNK, (c + 1) * GLA_CHUNK)
        la = log_a[rs, :]
        bcum = jnp.dot(tri, la, precision=lax.Precision.HIGHEST, preferred_element_type=F32)
        bl = bcum[GLA_CHUNK - 1:GLA_CHUNK, :]
        kf = k_ref[rs, :].astype(F32)
        qd = (q_ref[rs, :].astype(F32) * (GLA_DK ** -0.5)) * jnp.exp(bcum)
        ki = kf * jnp.exp(-bcum)
        ke = kf * jnp.exp(bl - bcum)
        gl = jnp.exp(bl)
        for h in range(GLA_HEADS):
            hs = slice(h * GLA_DK, (h + 1) * GLA_DK)
            vs = slice(h * GLA_DV, (h + 1) * GLA_DV)
            qd_h = qd[:, hs].astype(BF16)
            sc = lax.dot_general(qd_h, ki[:, hs].astype(BF16), nt, preferred_element_type=F32)
            sc = jnp.where(causal, sc, 0.0)
            v_h = v_ref[rs, vs]
            st_t = st_ref[h]
            o = jnp.dot(sc.astype(BF16), v_h, preferred_element_type=F32)
            o = o + lax.dot_general(qd_h, st_t.astype(BF16), nt, preferred_element_type=F32)
            st_ref[h] = st_t * gl[:, hs] + lax.dot_general(
                v_h, ke[:, hs].astype(BF16), tn, preferred_element_type=F32)
            ms = jnp.mean(o * o, axis=-1, keepdims=True)
            o = o * lax.rsqrt(ms + RMS_EPS) * ng_ref[:, vs]
            o_ref[rs, vs] = (o * _silu(r_ref[rs, vs].astype(F32))).astype(BF16)


def _gla(p, wup, bg, ng, tri, batch, seq, ts):
    t = batch * seq
    nst = seq // ts
    row = lambda b, s: b * nst + s
    full = lambda shape: pl.BlockSpec(shape, lambda b, s: (0,) * len(shape))
    q_blk = (2 * D_LRU) // GLA_QK_DIM
    v_blk = (2 * D_LRU + 2 * GLA_QK_DIM) // GLA_V_DIM
    gl_blk = N_MAIN // GLOW_PAD
    return pl.pallas_call(
        functools.partial(_gla_kernel, ts=ts),
        grid=(batch, nst),
        in_specs=[pl.BlockSpec((ts, GLA_QK_DIM), lambda b, s: (row(b, s), q_blk)),
                  pl.BlockSpec((ts, GLA_QK_DIM), lambda b, s: (row(b, s), q_blk + 1)),
                  pl.BlockSpec((ts, GLA_V_DIM), lambda b, s: (row(b, s), v_blk)),
                  pl.BlockSpec((ts, GLA_V_DIM), lambda b, s: (row(b, s), v_blk + 1)),
                  pl.BlockSpec((ts, GLOW_PAD), lambda b, s: (row(b, s), gl_blk)),
                  full((GLOW_PAD, GLA_QK_DIM)), full((1, GLA_QK_DIM)), full((1, GLA_V_DIM)),
                  full((GLA_CHUNK, GLA_CHUNK))],
        out_specs=pl.BlockSpec((ts, GLA_V_DIM), lambda b, s: (row(b, s), 0)),
        out_shape=jax.ShapeDtypeStruct((t, GLA_V_DIM), BF16),
        scratch_shapes=[pltpu.VMEM((GLA_HEADS, GLA_DV, GLA_DK), F32)],
        compiler_params=_cparams(("arbitrary", "arbitrary"), 40),
        name="gla",
    )(p, p, p, p, p, wup, bg, ng, tri)


def _merge_kernel(lru_ref, gla_ref, ga_ref, gb_ref, x_ref, wa_ref, wb_ref, wo_ref, g_ref, b_ref,
                  h_ref, hpa_ref, hpb_ref):
    ya = jnp.dot(lru_ref[...], wa_ref[...], preferred_element_type=F32)
    yb = jnp.dot(gla_ref[...], wb_ref[...], preferred_element_type=F32)
    merged = _sigmoid(ga_ref[...].astype(F32)) * ya + _sigmoid(gb_ref[...].astype(F32)) * yb
    mix = jnp.dot(merged.astype(BF16), wo_ref[...], preferred_element_type=F32)
    h = _layer_norm(DN_ALPHA * x_ref[...] + mix, g_ref[...], b_ref[...])
    h_ref[...] = h
    hp = _pack_rows(h)
    quarter = hp.shape[1] // 2
    hpa_ref[...] = hp[:, :quarter]
    hpb_ref[...] = hp[:, quarter:]


def _merge(lru_o, gla_o, p, x2, wa, wb, wo, g, b, tm):
    t, d = x2.shape
    ga_blk = (2 * D_LRU + 2 * GLA_QK_DIM + 2 * GLA_V_DIM) // D_MODEL
    rowb = lambda c: pl.BlockSpec((tm, d), lambda i: (i, c))
    full = lambda shape: pl.BlockSpec(shape, lambda i: (0,) * len(shape),
                                      pipeline_mode=pl.Buffered(1))
    return pl.pallas_call(
        _merge_kernel,
        grid=(t // tm,),
        in_specs=[rowb(0), rowb(0), rowb(ga_blk), rowb(ga_blk + 1), rowb(0),
                  full((d, d)), full((d, d)), full((d, d)), full((1, d)), full((1, d))],
        out_specs=[rowb(0)] + [pl.BlockSpec((tm, d // 4), lambda i: (i, 0))] * 2,
        out_shape=[jax.ShapeDtypeStruct((t, d), F32)]
                  + [jax.ShapeDtypeStruct((t, d // 4), U32)] * 2,
        compiler_params=_cparams(("arbitrary",), 48),
        name="merge",
    )(lru_o, gla_o, p, p, x2, wa, wb, wo, g, b)


def _route_kernel(h_ref, wrt_ref, bias_ref, upper_ref, idx_ref, w_ref, rank_ref, cnt_ref,
                  carry, *, tr):
    @pl.when(pl.program_id(0) == 0)
    def _init():
        carry[...] = jnp.zeros(carry.shape, F32)

    neg = -jnp.inf
    nt = (((1,), (1,)), ((), ()))
    logits = lax.dot_general(wrt_ref[...], h_ref[...], nt, precision=lax.Precision.HIGHEST,
                             preferred_element_type=F32)
    scores = _sigmoid(logits)
    biased = scores + bias_ref[...]

    gs = []
    for g in range(N_GROUPS):
        blk = biased[g * GROUP_SIZE:(g + 1) * GROUP_SIZE, :]
        m1 = jnp.max(blk, axis=0, keepdims=True)
        eq = blk == m1
        n1 = jnp.sum(eq.astype(F32), axis=0, keepdims=True)
        m2 = jnp.max(jnp.where(eq, neg, blk), axis=0, keepdims=True)
        gs.append(m1 + jnp.where(n1 >= 2.0, m1, m2))
    sel = [jnp.zeros((1, tr), jnp.bool_) for _ in range(N_GROUPS)]
    for _ in range(TOPK_GROUPS):
        cur = [jnp.where(sel[g], neg, gs[g]) for g in range(N_GROUPS)]
        m = cur[0]
        for g in range(1, N_GROUPS):
            m = jnp.maximum(m, cur[g])
        found = jnp.zeros((1, tr), jnp.bool_)
        for g in range(N_GROUPS):
            pick = jnp.logical_and(cur[g] == m, jnp.logical_not(found))
            sel[g] = jnp.logical_or(sel[g], pick)
            found = jnp.logical_or(found, pick)
    cur = jnp.concatenate(
        [jnp.where(sel[g], biased[g * GROUP_SIZE:(g + 1) * GROUP_SIZE, :], neg)
         for g in range(N_GROUPS)], axis=0)

    rowid = lax.broadcasted_iota(I32, (N_EXPERTS, tr), 0)
    picked = jnp.zeros((N_EXPERTS, tr), jnp.bool_)
    idxs, ws = [], []
    for _ in range(TOP_K):
        m = jnp.max(cur, axis=0, keepdims=True)
        first = jnp.min(jnp.where(cur == m, rowid, N_EXPERTS), axis=0, keepdims=True)
        onehot = rowid == first
        ws.append(jnp.sum(jnp.where(onehot, scores, 0.0), axis=0, keepdims=True))
        idxs.append(first)
        cur = jnp.where(onehot, neg, cur)
        picked = jnp.logical_or(picked, onehot)
    wsum = ws[0]
    for k in range(1, TOP_K):
        wsum = wsum + ws[k]

    pf = picked.astype(F32)
    before = jnp.dot(pf.astype(BF16), upper_ref[...], preferred_element_type=F32) + carry[...]
    for k in range(TOP_K):
        idx_ref[k:k + 1, :] = idxs[k]
        w_ref[k:k + 1, :] = ws[k] / wsum * ROUTED_SCALE
        rank_ref[k:k + 1, :] = jnp.sum(jnp.where(rowid == idxs[k], before, 0.0), axis=0,
                                       keepdims=True).astype(I32)
    carry[...] = carry[...] + jnp.sum(pf, axis=1, keepdims=True)
    cnt_ref[...] = carry[...]


def _route(h, wrt, bias_col, upper, tr):
    t, d = h.shape
    full = lambda shape: pl.BlockSpec(shape, lambda i: (0,) * len(shape))
    tok = lambda: pl.BlockSpec((TOP_K, tr), lambda i: (0, i))
    return pl.pallas_call(
        functools.partial(_route_kernel, tr=tr),
        grid=(t // tr,),
        in_specs=[pl.BlockSpec((tr, d), lambda i: (i, 0)),
                  full((N_EXPERTS, d)), full((N_EXPERTS, 1)), full((tr, tr))],
        out_specs=[tok(), tok(), tok(), full((N_EXPERTS, 1))],
        out_shape=[jax.ShapeDtypeStruct((TOP_K, t), I32), jax.ShapeDtypeStruct((TOP_K, t), F32),
                   jax.ShapeDtypeStruct((TOP_K, t), I32),
                   jax.ShapeDtypeStruct((N_EXPERTS, 1), F32)],
        scratch_shapes=[pltpu.VMEM((N_EXPERTS, 1), F32)],
        compiler_params=_cparams(("arbitrary",), 32),
        name="route",
    )(h, wrt, bias_col, upper)


def _slots_kernel(idx_ref, rank_ref, pstart_ref, dest_ref, *, tl):
    rowid = lax.broadcasted_iota(I32, (N_EXPERTS, tl), 0)
    pstart = pstart_ref[...]
    for k in range(TOP_K):
        base = jnp.sum(jnp.where(rowid == idx_ref[k:k + 1, :], pstart, 0.0), axis=0, keepdims=True)
        dest_ref[k:k + 1, :] = base.astype(I32) + rank_ref[k:k + 1, :]


def _slots(idx, rank, pstart_col, tl):
    t = idx.shape[1]
    tok = lambda: pl.BlockSpec((TOP_K, tl), lambda i: (0, i))
    return pl.pallas_call(
        functools.partial(_slots_kernel, tl=tl),
        grid=(t // tl,),
        in_specs=[tok(), tok(), pl.BlockSpec((N_EXPERTS, 1), lambda i: (0, 0))],
        out_specs=tok(),
        out_shape=jax.ShapeDtypeStruct((TOP_K, t), I32),
        compiler_params=_cparams(("arbitrary",), 32),
        name="slots",
    )(idx, rank, pstart_col)


def _sc_scatter_rows(src, idx, n_out):
    n = idx.shape[0]
    t, dp = src.shape
    nsrc = t // SC_WINDOW
    mesh = plsc.VectorSubcoreMesh(core_axis_name="core", subcore_axis_name="subcore")

    @pl.kernel(out_type=jax.ShapeDtypeStruct((n_out, dp), src.dtype), mesh=mesh,
               name="sc_scatter")
    def _scatter(src_hbm, idx_hbm, out_hbm):
        def _body(src_vmem, idx_vmem):
            pltpu.sync_copy(src_vmem, out_hbm.at[idx_vmem.at[0]])

        pltpu.emit_pipeline(
            _body,
            grid=(n // SC_WINDOW,),
            in_specs=[pl.BlockSpec((SC_WINDOW, dp), lambda i: (i % nsrc, 0)),
                      pl.BlockSpec((1, SC_WINDOW), lambda i: (0, i))],
            out_specs=[],
            core_axis_name=("core", "subcore"),
            dimension_semantics=(pltpu.PARALLEL,),
        )(src_hbm, idx_hbm)

    return _scatter(src, idx.reshape(1, n))


def _expert_kernel(bmap_ref, bexp_ref, flag_ref, nrow_ref, xa_ref, xb_ref, wg_ref, wu_ref, wd_ref,
                   ya_ref, yb_ref, wgu_s, wd_s):
    i = pl.program_id(0)
    flags = flag_ref[i]
    half = D_MODEL // 2

    @pl.when(flags >= 2)
    def _cast():
        wgu_s[:, :D_EXPERT] = wg_ref[...].astype(BF16)
        wgu_s[:, D_EXPERT:] = wu_ref[...].astype(BF16)
        wd_s[...] = wd_ref[...].astype(BF16)

    @pl.when(flags % 2 == 1)
    def _compute():
        live = lax.broadcasted_iota(I32, xa_ref.shape, 0) < nrow_ref[i]
        lo_a, hi_a = _unpack_rows(jnp.where(live, xa_ref[...], jnp.uint32(0)))
        lo_b, hi_b = _unpack_rows(jnp.where(live, xb_ref[...], jnp.uint32(0)))
        x = jnp.concatenate([lo_a.astype(BF16), lo_b.astype(BF16), hi_a.astype(BF16),
                             hi_b.astype(BF16)], axis=1)
        z = jnp.dot(x, wgu_s[...], preferred_element_type=F32)
        act = _silu(z[:, :D_EXPERT]) * z[:, D_EXPERT:]
        yp = _pack_rows(jnp.dot(act.astype(BF16), wd_s[...], preferred_element_type=F32))
        ya_ref[...] = yp[:, :half // 2]
        yb_ref[...] = yp[:, half // 2:]

    @pl.when(flags % 2 == 0)
    def _tail():
        ya_ref[...] = jnp.zeros(ya_ref.shape, ya_ref.dtype)
        yb_ref[...] = jnp.zeros(yb_ref.shape, yb_ref.dtype)


def _experts(bmap, bexp, flags, nrow, xa, xb, wg, wu, wd):
    n_slots, dq = xa.shape
    d = 4 * dq
    nb = n_slots // EXPERT_ROWS
    grid_spec = pltpu.PrefetchScalarGridSpec(
        num_scalar_prefetch=4,
        grid=(nb,),
        in_specs=[pl.BlockSpec((EXPERT_ROWS, dq), lambda i, bm, be, fl, nr: (bm[i], 0)),
                  pl.BlockSpec((EXPERT_ROWS, dq), lambda i, bm, be, fl, nr: (bm[i], 0)),
                  pl.BlockSpec((None, d, D_EXPERT), lambda i, bm, be, fl, nr: (be[i], 0, 0)),
                  pl.BlockSpec((None, d, D_EXPERT), lambda i, bm, be, fl, nr: (be[i], 0, 0)),
                  pl.BlockSpec((None, D_EXPERT, d), lambda i, bm, be, fl, nr: (be[i], 0, 0))],
        out_specs=[pl.BlockSpec((EXPERT_ROWS, dq), lambda i, bm, be, fl, nr: (i, 0))] * 2,
        scratch_shapes=[pltpu.VMEM((d, 2 * D_EXPERT), BF16), pltpu.VMEM((D_EXPERT, d), BF16)],
    )
    return pl.pallas_call(
        _expert_kernel,
        grid_spec=grid_spec,
        out_shape=[jax.ShapeDtypeStruct((n_slots, dq), U32)] * 2,
        compiler_params=_cparams(("arbitrary",), 32),
        name="experts",
    )(bmap, bexp, flags, nrow, xa, xb, wg, wu, wd)


def _sc_gather_rows(table, idx):
    n = idx.shape[0]
    dp = table.shape[1]
    mesh = plsc.VectorSubcoreMesh(core_axis_name="core", subcore_axis_name="subcore")

    @pl.kernel(out_type=jax.ShapeDtypeStruct((n, dp), table.dtype), mesh=mesh, name="sc_gather")
    def _gather(table_hbm, idx_hbm, out_hbm):
        def _body(idx_vmem, out_vmem):
            pltpu.sync_copy(table_hbm.at[idx_vmem.at[0]], out_vmem)

        pltpu.emit_pipeline(
            _body,
            grid=(n // SC_WINDOW,),
            in_specs=[pl.BlockSpec((1, SC_WINDOW), lambda i: (0, i))],
            out_specs=[pl.BlockSpec((SC_WINDOW, dp), lambda i: (i, 0))],
            core_axis_name=("core", "subcore"),
            dimension_semantics=(pltpu.PARALLEL,),
        )(idx_hbm, out_hbm)

    return _gather(table, idx.reshape(1, n))


def _combine_kernel(h_ref, w_ref, *rest):
    ya_refs = rest[:TOP_K]
    yb_refs = rest[TOP_K:2 * TOP_K]
    wsgu_ref, wsd_ref, g_ref, b_ref, o_ref = rest[2 * TOP_K:]
    h = h_ref[...]
    z = jnp.dot(h.astype(BF16), wsgu_ref[...], preferred_element_type=F32)
    act = _silu(z[:, :D_SHARED]) * z[:, D_SHARED:]
    ffn = jnp.dot(act.astype(BF16), wsd_ref[...], preferred_element_type=F32)
    acc = [None] * 4
    for k in range(TOP_K):
        wk = w_ref[:, k:k + 1]
        parts = _unpack_rows(ya_refs[k][...]) + _unpack_rows(yb_refs[k][...])
        for j in range(4):
            acc[j] = parts[j] * wk if k == 0 else acc[j] + parts[j] * wk
    ffn = ffn + jnp.concatenate([acc[0], acc[2], acc[1], acc[3]], axis=1)
    o_ref[...] = _layer_norm(DN_ALPHA * h + ffn, g_ref[...], b_ref[...])


def _combine(h, w_t, yga, ygb, wsgu, wsd, g, b, tm):
    t, d = h.shape
    nt = t // tm
    full = lambda shape: pl.BlockSpec(shape, lambda i: (0,) * len(shape))
    y_spec = lambda k: pl.BlockSpec((tm, d // 4), lambda i: (k * nt + i, 0))
    return pl.pallas_call(
        _combine_kernel,
        grid=(nt,),
        in_specs=[pl.BlockSpec((tm, d), lambda i: (i, 0)),
                  pl.BlockSpec((tm, TOP_K), lambda i: (i, 0))]
                 + [y_spec(k) for k in range(TOP_K)] * 2
                 + [full((d, 2 * D_SHARED)), full((D_SHARED, d)), full((1, d)), full((1, d))],
        out_specs=pl.BlockSpec((tm, d), lambda i: (i, 0)),
        out_shape=jax.ShapeDtypeStruct((t, d), F32),
        compiler_params=_cparams(("arbitrary",), 40),
        name="combine",
    )(h, w_t, *([yga] * TOP_K), *([ygb] * TOP_K), wsgu, wsd, g, b)


def _block_diag4(w):
    n = w.shape[0] // 4
    w4 = w.reshape(n, 4, LRU_BLOCK_DIM, LRU_BLOCK_DIM)
    eye = jnp.eye(4, dtype=w.dtype)
    return jnp.einsum('gaij,ab->gaibj', w4, eye).reshape(n, 4 * LRU_BLOCK_DIM, 4 * LRU_BLOCK_DIM)


def _layer(x, w_in, conv_w, conv_b, lru_w_a, lru_b_a, lru_w_x, lru_b_x, lru_lambda, w_lru_out,
           gla_w_gate_up, gla_b_gate, gla_norm_g, w_gla_out, w_mix_out, ln1_g, ln1_b,
           w_router, router_bias, w_exp_gate, w_exp_up, w_exp_down, w_sh_gate, w_sh_up, w_sh_down,
           ln2_g, ln2_b):
    batch, seq, d = x.shape
    t = batch * seq
    x2 = x.reshape(t, d)
    row = lambda v: v.reshape(1, -1)

    glow_lo = 2 * D_LRU + 2 * GLA_QK_DIM + 2 * GLA_V_DIM
    glow_hi = glow_lo + GLA_GATE_RANK
    w_p = jnp.concatenate(
        [w_in[:, :glow_lo], w_in[:, glow_hi:], w_in[:, glow_lo:glow_hi],
         jnp.zeros((d, GLOW_PAD - GLA_GATE_RANK), w_in.dtype)], axis=1).astype(BF16)
    p = _proj(x2, w_p, tm=min(512, t))

    wg = jnp.concatenate([_block_diag4(lru_w_a), _block_diag4(lru_w_x)], axis=2).astype(BF16)
    ts_lru = min(256, seq)
    lru_o = _lru(p, conv_w, row(conv_b), wg, row(lru_b_a), row(lru_b_x), row(lru_lambda),
                 batch, seq, ts_lru)

    wup = jnp.concatenate(
        [gla_w_gate_up, jnp.zeros((GLOW_PAD - GLA_GATE_RANK, GLA_QK_DIM), gla_w_gate_up.dtype)],
        axis=0).astype(BF16)
    tri = jnp.tril(jnp.ones((GLA_CHUNK, GLA_CHUNK), F32))
    ts_gla = min(256, seq)
    gla_o = _gla(p, wup, row(gla_b_gate), row(gla_norm_g), tri, batch, seq, ts_gla)

    h, hpa, hpb = _merge(lru_o, gla_o, p, x2, w_lru_out.astype(BF16), w_gla_out.astype(BF16),
                         w_mix_out.astype(BF16), row(ln1_g), row(ln1_b), tm=min(512, t))

    tr = min(256, t)
    upper = jnp.triu(jnp.ones((tr, tr), F32), k=1).astype(BF16)
    idx, w, rank, cnt = _route(h, w_router.T, router_bias.reshape(-1, 1), upper, tr)

    counts = cnt[:, 0].astype(I32)
    padded = (counts + EXPERT_ROWS - 1) // EXPERT_ROWS * EXPERT_ROWS
    pend = jnp.cumsum(padded)
    pstart = pend - padded
    dest = _slots(idx, rank, pstart.astype(F32).reshape(-1, 1), tl=min(512, t))
    n_blocks = (t * TOP_K) // EXPERT_ROWS + N_EXPERTS
    n_slots = n_blocks * EXPERT_ROWS
    blk = jnp.arange(n_blocks, dtype=I32)
    n_used = pend[-1] // EXPERT_ROWS
    bmap = jnp.minimum(blk, n_used - 1)
    bexp = jnp.minimum(
        jnp.sum((pend[None, :] <= (bmap * EXPERT_ROWS)[:, None]).astype(I32), axis=1),
        N_EXPERTS - 1)
    first = jnp.concatenate([jnp.ones((1,), I32), (bexp[1:] != bexp[:-1]).astype(I32)])
    flags = (blk < n_used).astype(I32) + 2 * first

    live_end = jnp.sum(jnp.where(bexp[:, None] == jnp.arange(N_EXPERTS, dtype=I32)[None, :],
                                 (pstart + counts)[None, :], 0), axis=1)
    nrow = jnp.clip(live_end - bmap * EXPERT_ROWS, 0, EXPERT_ROWS).astype(I32)

    dest_flat = dest.reshape(-1)
    xa = _sc_scatter_rows(hpa, dest_flat, n_slots)
    xb = _sc_scatter_rows(hpb, dest_flat, n_slots)
    ya, yb = _experts(bmap, bexp, flags, nrow, xa, xb, w_exp_gate, w_exp_up, w_exp_down)

    wsgu = jnp.concatenate([w_sh_gate, w_sh_up], axis=1).astype(BF16)
    out = _combine(h, w.T, _sc_gather_rows(ya, dest_flat), _sc_gather_rows(yb, dest_flat), wsgu,
                   w_sh_down.astype(BF16), row(ln2_g), row(ln2_b), tm=min(512, t))
    return out.reshape(batch, seq, d)


def kernel(x, w_in, conv_w, conv_b, lru_w_a, lru_b_a, lru_w_x, lru_b_x, lru_lambda, w_lru_out,
           gla_w_gate_up, gla_b_gate, gla_norm_g, w_gla_out, w_mix_out, ln1_g, ln1_b,
           w_router, router_bias, w_exp_gate, w_exp_up, w_exp_down, w_sh_gate, w_sh_up, w_sh_down,
           ln2_g, ln2_b):
    params = (w_in, conv_w, conv_b, lru_w_a, lru_b_a, lru_w_x, lru_b_x, lru_lambda, w_lru_out,
              gla_w_gate_up, gla_b_gate, gla_norm_g, w_gla_out, w_mix_out, ln1_g, ln1_b,
              w_router, router_bias, w_exp_gate, w_exp_up, w_exp_down, w_sh_gate, w_sh_up,
              w_sh_down, ln2_g, ln2_b)
    h = x
    for l in range(DEPTH):
        h = _layer(h, *(p[l] for p in params))
    return h
```

```python
import functools

import jax
import jax.numpy as jnp
from jax import lax
from jax.experimental import pallas as pl
from jax.experimental.pallas import tpu as pltpu
from jax.experimental.pallas import tpu_sc as plsc

F32 = jnp.float32
BF16 = jnp.bfloat16
I32 = jnp.int32
U32 = jnp.uint32

D_MODEL = 1024
D_LRU = D_MODEL
LRU_BLOCKS = 16
LRU_BLOCK_DIM = D_LRU // LRU_BLOCKS
CONV_WIDTH = 4
LRU_C = 8.0
GLA_HEADS = 4
GLA_QK_DIM = D_MODEL // 2
GLA_V_DIM = D_MODEL
GLA_DK = GLA_QK_DIM // GLA_HEADS
GLA_DV = GLA_V_DIM // GLA_HEADS
GLA_GATE_RANK = 16
GLA_TAU = 16.0
GLA_CHUNK = 64
N_EXPERTS = 256
TOP_K = 8
N_GROUPS = 8
GROUP_SIZE = N_EXPERTS // N_GROUPS
TOPK_GROUPS = 4
D_EXPERT = 256
D_SHARED = 256
ROUTED_SCALE = 2.5
DEPTH = 1
DN_ALPHA = (2.0 * DEPTH) ** 0.25
LN_EPS = 1e-5
RMS_EPS = 1e-6

LANES = 128
SUBLANES = 8
GLOW_PAD = LANES
N_MAIN = 2 * D_LRU + 2 * GLA_QK_DIM + 2 * GLA_V_DIM + 2 * D_MODEL
N_PROJ = N_MAIN + GLOW_PAD
PROJ_CHUNK = 512

EXPERT_ROWS = 512
SC_WINDOW = 128


def _cparams(semantics, vmem_mib):
    return pltpu.CompilerParams(dimension_semantics=semantics,
                                vmem_limit_bytes=vmem_mib * 1024 * 1024)


def _sigmoid(x):
    return jax.nn.sigmoid(x)


def _silu(x):
    return x * jax.nn.sigmoid(x)


def _gelu_tanh(x):
    c = 0.7978845608028654
    return x * (0.5 * (1.0 + jnp.tanh(c * (x + 0.044715 * (x * x * x)))))


def _pack_rows(v):
    n = v.shape[1] // 2
    lo = lax.bitcast_convert_type(v[:, :n].astype(BF16).astype(F32), U32)
    hi = lax.bitcast_convert_type(v[:, n:].astype(BF16).astype(F32), U32)
    return (lo >> 16) | (hi & jnp.uint32(0xFFFF0000))


def _unpack_rows(p):
    lo = lax.bitcast_convert_type(p << 16, F32)
    hi = lax.bitcast_convert_type(p & jnp.uint32(0xFFFF0000), F32)
    return lo, hi


def _layer_norm(z, g, b):
    mu = jnp.mean(z, axis=-1, keepdims=True)
    zc = z - mu
    var = jnp.mean(zc * zc, axis=-1, keepdims=True)
    return zc * lax.rsqrt(var + LN_EPS) * g + b


def _proj_kernel(x_ref, w_ref, o_ref):
    xb = x_ref[...].astype(BF16)
    for j in range(0, N_MAIN, PROJ_CHUNK):
        o_ref[:, j:j + PROJ_CHUNK] = jnp.dot(
            xb, w_ref[:, j:j + PROJ_CHUNK], preferred_element_type=F32).astype(BF16)
    o_ref[:, N_MAIN:] = jnp.dot(xb, w_ref[:, N_MAIN:], preferred_element_type=F32).astype(BF16)


def _proj(x2, w_p, tm):
    t, d = x2.shape
    return pl.pallas_call(
        _proj_kernel,
        grid=(t // tm,),
        in_specs=[pl.BlockSpec((tm, d), lambda i: (i, 0)),
                  pl.BlockSpec((d, N_PROJ), lambda i: (0, 0), pipeline_mode=pl.Buffered(1))],
        out_specs=pl.BlockSpec((tm, N_PROJ), lambda i: (i, 0)),
        out_shape=jax.ShapeDtypeStruct((t, N_PROJ), BF16),
        compiler_params=_cparams(("arbitrary",), 48),
        name="proj",
    )(x2, w_p)


def _lru_kernel(xa_ref, ga_ref, cw_ref, cb_ref, wg_ref, ba_ref, bx_ref, lam_ref, o_ref,
                xbuf, xc_s, a_s, u_s, cin_s, hnat, carry, *, ts):
    nb = ts // SUBLANES
    nslab = D_LRU // LANES
    grp = 4 * LRU_BLOCK_DIM
    hist = SUBLANES

    @pl.when(pl.program_id(1) == 0)
    def _init():
        for s in range(nslab):
            xbuf[s, 0:hist, :] = jnp.zeros((hist, LANES), F32)
        carry[...] = jnp.zeros((1, D_LRU), F32)

    for s in range(nslab):
        cs_ = slice(s * LANES, (s + 1) * LANES)
        xbuf[s, hist:, :] = xa_ref[:, cs_].astype(F32)
        taps = {}
        for r in range(SUBLANES):
            acc = cb_ref[:, cs_]
            for j in range(CONV_WIDTH):
                off = hist - (CONV_WIDTH - 1) + r + j
                if off not in taps:
                    taps[off] = xbuf[s, pl.ds(off, nb, stride=SUBLANES), :]
                acc = acc + cw_ref[j:j + 1, cs_] * taps[off]
            xc_s[r * nb:(r + 1) * nb, cs_] = acc
        xbuf[s, 0:hist, :] = xbuf[s, ts:ts + hist, :]

    lam = lam_ref[...]
    sp = jnp.maximum(-lam, 0.0) + jnp.log1p(jnp.exp(-jnp.abs(lam)))
    for g in range(D_LRU // grp):
        sl = slice(g * grp, (g + 1) * grp)
        xg = xc_s[:, sl]
        z = jnp.dot(xg.astype(BF16), wg_ref[g], preferred_element_type=F32)
        r = _sigmoid(z[:, :grp] + ba_ref[:, sl])
        i = _sigmoid(z[:, grp:] + bx_ref[:, sl])
        log_a = (-LRU_C * r) * sp[:, sl]
        a = jnp.exp(log_a)
        a_s[:, sl] = a
        u_s[:, sl] = jnp.sqrt(-jnp.tanh(log_a) * (a * a + 1.0)) * (i * xg)

    for r in range(1, SUBLANES):
        prev = slice((r - 1) * nb, r * nb)
        cur = slice(r * nb, (r + 1) * nb)
        a_r = a_s[cur, :]
        u_s[cur, :] = a_r * u_s[prev, :] + u_s[cur, :]
        a_s[cur, :] = a_r * a_s[prev, :]

    last = (SUBLANES - 1) * nb

    def _carry_step(b, c):
        cin_s[pl.ds(b, 1), :] = c
        return a_s[pl.ds(last + b, 1), :] * c + u_s[pl.ds(last + b, 1), :]

    carry[...] = lax.fori_loop(0, nb, _carry_step, carry[...])

    cin = cin_s[...]
    for r in range(SUBLANES):
        cur = slice(r * nb, (r + 1) * nb)
        h_r = u_s[cur, :] + a_s[cur, :] * cin
        for s in range(nslab):
            hnat[s, pl.ds(r, nb, stride=SUBLANES), :] = h_r[:, s * LANES:(s + 1) * LANES]

    for s in range(nslab):
        cs_ = slice(s * LANES, (s + 1) * LANES)
        o_ref[:, cs_] = (hnat[s] * _gelu_tanh(ga_ref[:, cs_].astype(F32))).astype(BF16)


def _lru(p, cw, cb, wg, ba, bx, lam, batch, seq, ts):
    t = batch * seq
    nst = seq // ts
    nslab = D_LRU // LANES
    row = lambda b, s: b * nst + s
    full = lambda shape: pl.BlockSpec(shape, lambda b, s: (0,) * len(shape))
    return pl.pallas_call(
        functools.partial(_lru_kernel, ts=ts),
        grid=(batch, nst),
        in_specs=[pl.BlockSpec((ts, D_LRU), lambda b, s: (row(b, s), 0)),
                  pl.BlockSpec((ts, D_LRU), lambda b, s: (row(b, s), 1)),
                  full((CONV_WIDTH, D_LRU)), full((1, D_LRU)),
                  full(wg.shape), full((1, D_LRU)), full((1, D_LRU)), full((1, D_LRU))],
        out_specs=pl.BlockSpec((ts, D_LRU), lambda b, s: (row(b, s), 0)),
        out_shape=jax.ShapeDtypeStruct((t, D_LRU), BF16),
        scratch_shapes=[pltpu.VMEM((nslab, ts + SUBLANES, LANES), F32),
                        pltpu.VMEM((ts, D_LRU), F32),
                        pltpu.VMEM((ts, D_LRU), F32),
                        pltpu.VMEM((ts, D_LRU), F32),
                        pltpu.VMEM((ts // SUBLANES, D_LRU), F32),
                        pltpu.VMEM((nslab, ts, LANES), F32),
                        pltpu.VMEM((1, D_LRU), F32)],
        compiler_params=_cparams(("arbitrary", "arbitrary"), 40),
        name="lru",
    )(p, p, cw, cb, wg, ba, bx, lam)


def _gla_kernel(q_ref, k_ref, v_ref, r_ref, gl_ref, wup_ref, bg_ref, ng_ref, tri_ref, o_ref,
                st_ref, *, ts):
    @pl.when(pl.program_id(1) == 0)
    def _init():
        st_ref[...] = jnp.zeros(st_ref.shape, F32)

    logit = jnp.dot(gl_ref[...], wup_ref[...], preferred_element_type=F32) + bg_ref[...]
    log_a = (jnp.minimum(logit, 0.0) - jnp.log1p(jnp.exp(-jnp.abs(logit)))) * (1.0 / GLA_TAU)
    tri = tri_ref[...]
    ri = lax.broadcasted_iota(I32, (GLA_CHUNK, GLA_CHUNK), 0)
    ci = lax.broadcasted_iota(I32, (GLA_CHUNK, GLA_CHUNK), 1)
    causal = ri >= ci
    nt = (((1,), (1,)), ((), ()))
    tn = (((0,), (0,)), ((), ()))
    for c in range(ts // GLA_CHUNK):
        rs = slice(c * GLA_CHUNK, (c + 1) * GLA_CHUNK)
        la = log_a[rs, :]
        la_hi = la.astype(BF16)
        la_r = la - la_hi.astype(F32)
        la_mid = la_r.astype(BF16)
        la_lo = (la_r - la_mid.astype(F32)).astype(BF16)
        parts = jnp.dot(tri, jnp.concatenate([la_hi, la_mid, la_lo], axis=1),
                        preferred_element_type=F32)
        bcum = (parts[:, :GLA_QK_DIM] + parts[:, GLA_QK_DIM:2 * GLA_QK_DIM]
                + parts[:, 2 * GLA_QK_DIM:])
        bl = bcum[GLA_CHUNK - 1:GLA_CHUNK, :]
        kf = k_ref[rs, :].astype(F32)
        qd = (q_ref[rs, :].astype(F32) * (GLA_DK ** -0.5)) * jnp.exp(bcum)
        ki = kf * jnp.exp(-bcum)
        ke = kf * jnp.exp(bl - bcum)
        gl = jnp.exp(bl)
        for h in range(GLA_HEADS):
            hs = slice(h * GLA_DK, (h + 1) * GLA_DK)
            vs = slice(h * GLA_DV, (h + 1) * GLA_DV)
            qd_h = qd[:, hs].astype(BF16)
            sc = lax.dot_general(qd_h, ki[:, hs].astype(BF16), nt, preferred_element_type=F32)
            sc = jnp.where(causal, sc, 0.0)
            v_h = v_ref[rs, vs]
            st_t = st_ref[h]
            o = jnp.dot(sc.astype(BF16), v_h, preferred_element_type=F32)
            o = o + lax.dot_general(qd_h, st_t.astype(BF16), nt, preferred_element_type=F32)
            st_ref[h] = st_t * gl[:, hs] + lax.dot_general(
                v_h, ke[:, hs].astype(BF16), tn, preferred_element_type=F32)
            ms = jnp.mean(o * o, axis=-1, keepdims=True)
            o = o * lax.rsqrt(ms + RMS_EPS) * ng_ref[:, vs]
            o_ref[rs, vs] = (o * _silu(r_ref[rs, vs].astype(F32))).astype(BF16)


def _gla(p, wup, bg, ng, tri, batch, seq, ts):
    t = batch * seq
    nst = seq // ts
    row = lambda b, s: b * nst + s
    full = lambda shape: pl.BlockSpec(shape, lambda b, s: (0,) * len(shape))
    q_blk = (2 * D_LRU) // GLA_QK_DIM
    v_blk = (2 * D_LRU + 2 * GLA_QK_DIM) // GLA_V_DIM
    gl_blk = N_MAIN // GLOW_PAD
    return pl.pallas_call(
        functools.partial(_gla_kernel, ts=ts),
        grid=(batch, nst),
        in_specs=[pl.BlockSpec((ts, GLA_QK_DIM), lambda b, s: (row(b, s), q_blk)),
                  pl.BlockSpec((ts, GLA_QK_DIM), lambda b, s: (row(b, s), q_blk + 1)),
                  pl.BlockSpec((ts, GLA_V_DIM), lambda b, s: (row(b, s), v_blk)),
                  pl.BlockSpec((ts, GLA_V_DIM), lambda b, s: (row(b, s), v_blk + 1)),
                  pl.BlockSpec((ts, GLOW_PAD), lambda b, s: (row(b, s), gl_blk)),
                  full((GLOW_PAD, GLA_QK_DIM)), full((1, GLA_QK_DIM)), full((1, GLA_V_DIM)),
                  full((GLA_CHUNK, GLA_CHUNK))],
        out_specs=pl.BlockSpec((ts, GLA_V_DIM), lambda b, s: (row(b, s), 0)),
        out_shape=jax.ShapeDtypeStruct((t, GLA_V_DIM), BF16),
        scratch_shapes=[pltpu.VMEM((GLA_HEADS, GLA_DV, GLA_DK), F32)],
        compiler_params=_cparams(("arbitrary", "arbitrary"), 40),
        name="gla",
    )(p, p, p, p, p, wup, bg, ng, tri)


def _merge_kernel(lru_ref, gla_ref, ga_ref, gb_ref, x_ref, wa_ref, wb_ref, wo_ref, g_ref, b_ref,
                  h_ref, hpa_ref, hpb_ref):
    ya = jnp.dot(lru_ref[...], wa_ref[...], preferred_element_type=F32)
    yb = jnp.dot(gla_ref[...], wb_ref[...], preferred_element_type=F32)
    merged = _sigmoid(ga_ref[...].astype(F32)) * ya + _sigmoid(gb_ref[...].astype(F32)) * yb
    mix = jnp.dot(merged.astype(BF16), wo_ref[...], preferred_element_type=F32)
    h = _layer_norm(DN_ALPHA * x_ref[...] + mix, g_ref[...], b_ref[...])
    h_ref[...] = h
    hp = _pack_rows(h)
    quarter = hp.shape[1] // 2
    hpa_ref[...] = hp[:, :quarter]
    hpb_ref[...] = hp[:, quarter:]


def _merge(lru_o, gla_o, p, x2, wa, wb, wo, g, b, tm):
    t, d = x2.shape
    ga_blk = (2 * D_LRU + 2 * GLA_QK_DIM + 2 * GLA_V_DIM) // D_MODEL
    rowb = lambda c: pl.BlockSpec((tm, d), lambda i: (i, c))
    full = lambda shape: pl.BlockSpec(shape, lambda i: (0,) * len(shape),
                                      pipeline_mode=pl.Buffered(1))
    return pl.pallas_call(
        _merge_kernel,
        grid=(t // tm,),
        in_specs=[rowb(0), rowb(0), rowb(ga_blk), rowb(ga_blk + 1), rowb(0),
                  full((d, d)), full((d, d)), full((d, d)), full((1, d)), full((1, d))],
        out_specs=[rowb(0)] + [pl.BlockSpec((tm, d // 4), lambda i: (i, 0))] * 2,
        out_shape=[jax.ShapeDtypeStruct((t, d), F32)]
                  + [jax.ShapeDtypeStruct((t, d // 4), U32)] * 2,
        compiler_params=_cparams(("arbitrary",), 48),
        name="merge",
    )(lru_o, gla_o, p, p, x2, wa, wb, wo, g, b)


def _route_kernel(h_ref, wrt_ref, bias_ref, upper_ref, idx_ref, w_ref, rank_ref, cnt_ref,
                  carry, *, tr):
    @pl.when(pl.program_id(0) == 0)
    def _init():
        carry[...] = jnp.zeros(carry.shape, F32)

    neg = -jnp.inf
    nt = (((1,), (1,)), ((), ()))
    h = h_ref[...]
    h_hi = h.astype(BF16)
    h_lo = (h - h_hi.astype(F32)).astype(BF16)
    both = lax.dot_general(wrt_ref[...], h_hi, nt, preferred_element_type=F32)
    logits = (both[:N_EXPERTS, :] + both[N_EXPERTS:, :]
              + lax.dot_general(wrt_ref[:N_EXPERTS, :], h_lo, nt, preferred_element_type=F32))
    scores = _sigmoid(logits)
    biased = scores + bias_ref[...]

    gs = []
    for g in range(N_GROUPS):
        blk = biased[g * GROUP_SIZE:(g + 1) * GROUP_SIZE, :]
        m1 = jnp.max(blk, axis=0, keepdims=True)
        eq = blk == m1
        n1 = jnp.sum(eq.astype(F32), axis=0, keepdims=True)
        m2 = jnp.max(jnp.where(eq, neg, blk), axis=0, keepdims=True)
        gs.append(m1 + jnp.where(n1 >= 2.0, m1, m2))
    sel = [jnp.zeros((1, tr), jnp.bool_) for _ in range(N_GROUPS)]
    for _ in range(TOPK_GROUPS):
        cur = [jnp.where(sel[g], neg, gs[g]) for g in range(N_GROUPS)]
        m = cur[0]
        for g in range(1, N_GROUPS):
            m = jnp.maximum(m, cur[g])
        found = jnp.zeros((1, tr), jnp.bool_)
        for g in range(N_GROUPS):
            pick = jnp.logical_and(cur[g] == m, jnp.logical_not(found))
            sel[g] = jnp.logical_or(sel[g], pick)
            found = jnp.logical_or(found, pick)
    cur = jnp.concatenate(
        [jnp.where(sel[g], biased[g * GROUP_SIZE:(g + 1) * GROUP_SIZE, :], neg)
         for g in range(N_GROUPS)], axis=0)

    rowid = lax.broadcasted_iota(I32, (N_EXPERTS, tr), 0)
    eligible = cur != neg
    idxs, ws = [], []
    for _ in range(TOP_K):
        m = jnp.max(cur, axis=0, keepdims=True)
        first = jnp.min(jnp.where(cur == m, rowid, N_EXPERTS), axis=0, keepdims=True)
        onehot = rowid == first
        ws.append(jnp.sum(jnp.where(onehot, scores, 0.0), axis=0, keepdims=True))
        idxs.append(first)
        cur = jnp.where(onehot, neg, cur)
    picked = jnp.logical_and(eligible, cur == neg)
    wsum = ws[0]
    for k in range(1, TOP_K):
        wsum = wsum + ws[k]

    pf = picked.astype(F32)
    before = jnp.dot(pf.astype(BF16), upper_ref[...], preferred_element_type=F32) + carry[...]
    for k in range(TOP_K):
        idx_ref[k:k + 1, :] = idxs[k]
        w_ref[k:k + 1, :] = ws[k] / wsum * ROUTED_SCALE
        rank_ref[k:k + 1, :] = jnp.sum(jnp.where(rowid == idxs[k], before, 0.0), axis=0,
                                       keepdims=True).astype(I32)
    carry[...] = carry[...] + jnp.sum(pf, axis=1, keepdims=True)
    cnt_ref[...] = carry[...]


def _route(h, wrt, bias_col, upper, tr):
    t, d = h.shape
    full = lambda shape: pl.BlockSpec(shape, lambda i: (0,) * len(shape))
    tok = lambda: pl.BlockSpec((TOP_K, tr), lambda i: (0, i))
    return pl.pallas_call(
        functools.partial(_route_kernel, tr=tr),
        grid=(t // tr,),
        in_specs=[pl.BlockSpec((tr, d), lambda i: (i, 0)),
                  full((2 * N_EXPERTS, d)), full((N_EXPERTS, 1)), full((tr, tr))],
        out_specs=[tok(), tok(), tok(), full((N_EXPERTS, 1))],
        out_shape=[jax.ShapeDtypeStruct((TOP_K, t), I32), jax.ShapeDtypeStruct((TOP_K, t), F32),
                   jax.ShapeDtypeStruct((TOP_K, t), I32),
                   jax.ShapeDtypeStruct((N_EXPERTS, 1), F32)],
        scratch_shapes=[pltpu.VMEM((N_EXPERTS, 1), F32)],
        compiler_params=_cparams(("arbitrary",), 32),
        name="route",
    )(h, wrt, bias_col, upper)


def _slots_kernel(idx_ref, rank_ref, pstart_ref, dest_ref, *, tl):
    rowid = lax.broadcasted_iota(I32, (N_EXPERTS, tl), 0)
    pstart = pstart_ref[...]
    for k in range(TOP_K):
        base = jnp.sum(jnp.where(rowid == idx_ref[k:k + 1, :], pstart, 0.0), axis=0, keepdims=True)
        dest_ref[k:k + 1, :] = base.astype(I32) + rank_ref[k:k + 1, :]


def _slots(idx, rank, pstart_col, tl):
    t = idx.shape[1]
    tok = lambda: pl.BlockSpec((TOP_K, tl), lambda i: (0, i))
    return pl.pallas_call(
        functools.partial(_slots_kernel, tl=tl),
        grid=(t // tl,),
        in_specs=[tok(), tok(), pl.BlockSpec((N_EXPERTS, 1), lambda i: (0, 0))],
        out_specs=tok(),
        out_shape=jax.ShapeDtypeStruct((TOP_K, t), I32),
        compiler_params=_cparams(("arbitrary",), 32),
        name="slots",
    )(idx, rank, pstart_col)


def _sc_scatter_rows(src, idx, n_out):
    n = idx.shape[0]
    t, dp = src.shape
    nsrc = t // SC_WINDOW
    mesh = plsc.VectorSubcoreMesh(core_axis_name="core", subcore_axis_name="subcore")

    @pl.kernel(out_type=jax.ShapeDtypeStruct((n_out, dp), src.dtype), mesh=mesh,
               name="sc_scatter")
    def _scatter(src_hbm, idx_hbm, out_hbm):
        def _body(src_vmem, idx_vmem):
            pltpu.sync_copy(src_vmem, out_hbm.at[idx_vmem.at[0]])

        pltpu.emit_pipeline(
            _body,
            grid=(n // SC_WINDOW,),
            in_specs=[pl.BlockSpec((SC_WINDOW, dp), lambda i: (i % nsrc, 0)),
                      pl.BlockSpec((1, SC_WINDOW), lambda i: (0, i))],
            out_specs=[],
            core_axis_name=("core", "subcore"),
            dimension_semantics=(pltpu.PARALLEL,),
        )(src_hbm, idx_hbm)

    return _scatter(src, idx.reshape(1, n))


def _expert_kernel(bmap_ref, bexp_ref, flag_ref, nrow_ref, xa_ref, xb_ref, wg_ref, wu_ref, wd_ref,
                   ya_ref, yb_ref, wgu_s, wd_s):
    i = pl.program_id(0)
    flags = flag_ref[i]
    half = D_MODEL // 2

    @pl.when(flags >= 2)
    def _cast():
        wgu_s[:, :D_EXPERT] = wg_ref[...].astype(BF16)
        wgu_s[:, D_EXPERT:] = wu_ref[...].astype(BF16)
        wd_s[...] = wd_ref[...].astype(BF16)

    @pl.when(flags % 2 == 1)
    def _compute():
        live = lax.broadcasted_iota(I32, xa_ref.shape, 0) < nrow_ref[i]
        lo_a, hi_a = _unpack_rows(jnp.where(live, xa_ref[...], jnp.uint32(0)))
        lo_b, hi_b = _unpack_rows(jnp.where(live, xb_ref[...], jnp.uint32(0)))
        x = jnp.concatenate([lo_a.astype(BF16), lo_b.astype(BF16), hi_a.astype(BF16),
                             hi_b.astype(BF16)], axis=1)
        z = jnp.dot(x, wgu_s[...], preferred_element_type=F32)
        act = _silu(z[:, :D_EXPERT]) * z[:, D_EXPERT:]
        yp = _pack_rows(jnp.dot(act.astype(BF16), wd_s[...], preferred_element_type=F32))
        ya_ref[...] = yp[:, :half // 2]
        yb_ref[...] = yp[:, half // 2:]

    @pl.when(flags % 2 == 0)
    def _tail():
        ya_ref[...] = jnp.zeros(ya_ref.shape, ya_ref.dtype)
        yb_ref[...] = jnp.zeros(yb_ref.shape, yb_ref.dtype)


def _experts(bmap, bexp, flags, nrow, xa, xb, wg, wu, wd):
    n_slots, dq = xa.shape
    d = 4 * dq
    nb = n_slots // EXPERT_ROWS
    grid_spec = pltpu.PrefetchScalarGridSpec(
        num_scalar_prefetch=4,
        grid=(nb,),
        in_specs=[pl.BlockSpec((EXPERT_ROWS, dq), lambda i, bm, be, fl, nr: (bm[i], 0)),
                  pl.BlockSpec((EXPERT_ROWS, dq), lambda i, bm, be, fl, nr: (bm[i], 0)),
                  pl.BlockSpec((None, d, D_EXPERT), lambda i, bm, be, fl, nr: (be[i], 0, 0)),
                  pl.BlockSpec((None, d, D_EXPERT), lambda i, bm, be, fl, nr: (be[i], 0, 0)),
                  pl.BlockSpec((None, D_EXPERT, d), lambda i, bm, be, fl, nr: (be[i], 0, 0))],
        out_specs=[pl.BlockSpec((EXPERT_ROWS, dq), lambda i, bm, be, fl, nr: (i, 0))] * 2,
        scratch_shapes=[pltpu.VMEM((d, 2 * D_EXPERT), BF16), pltpu.VMEM((D_EXPERT, d), BF16)],
    )
    return pl.pallas_call(
        _expert_kernel,
        grid_spec=grid_spec,
        out_shape=[jax.ShapeDtypeStruct((n_slots, dq), U32)] * 2,
        compiler_params=_cparams(("arbitrary",), 32),
        name="experts",
    )(bmap, bexp, flags, nrow, xa, xb, wg, wu, wd)


def _sc_gather_rows(table, idx):
    n = idx.shape[0]
    dp = table.shape[1]
    mesh = plsc.VectorSubcoreMesh(core_axis_name="core", subcore_axis_name="subcore")

    @pl.kernel(out_type=jax.ShapeDtypeStruct((n, dp), table.dtype), mesh=mesh, name="sc_gather")
    def _gather(table_hbm, idx_hbm, out_hbm):
        def _body(idx_vmem, out_vmem):
            pltpu.sync_copy(table_hbm.at[idx_vmem.at[0]], out_vmem)

        pltpu.emit_pipeline(
            _body,
            grid=(n // SC_WINDOW,),
            in_specs=[pl.BlockSpec((1, SC_WINDOW), lambda i: (0, i))],
            out_specs=[pl.BlockSpec((SC_WINDOW, dp), lambda i: (i, 0))],
            core_axis_name=("core", "subcore"),
            dimension_semantics=(pltpu.PARALLEL,),
        )(idx_hbm, out_hbm)

    return _gather(table, idx.reshape(1, n))


def _combine_kernel(h_ref, w_ref, *rest):
    ya_refs = rest[:TOP_K]
    yb_refs = rest[TOP_K:2 * TOP_K]
    wsgu_ref, wsd_ref, g_ref, b_ref, o_ref = rest[2 * TOP_K:]
    h = h_ref[...]
    z = jnp.dot(h.astype(BF16), wsgu_ref[...], preferred_element_type=F32)
    act = _silu(z[:, :D_SHARED]) * z[:, D_SHARED:]
    ffn = jnp.dot(act.astype(BF16), wsd_ref[...], preferred_element_type=F32)
    acc = [None] * 4
    for k in range(TOP_K):
        wk = w_ref[:, k:k + 1]
        parts = _unpack_rows(ya_refs[k][...]) + _unpack_rows(yb_refs[k][...])
        for j in range(4):
            acc[j] = parts[j] * wk if k == 0 else acc[j] + parts[j] * wk
    ffn = ffn + jnp.concatenate([acc[0], acc[2], acc[1], acc[3]], axis=1)
    o_ref[...] = _layer_norm(DN_ALPHA * h + ffn, g_ref[...], b_ref[...])


def _combine(h, w_t, yga, ygb, wsgu, wsd, g, b, tm):
    t, d = h.shape
    nt = t // tm
    full = lambda shape: pl.BlockSpec(shape, lambda i: (0,) * len(shape))
    y_spec = lambda k: pl.BlockSpec((tm, d // 4), lambda i: (k * nt + i, 0))
    return pl.pallas_call(
        _combine_kernel,
        grid=(nt,),
        in_specs=[pl.BlockSpec((tm, d), lambda i: (i, 0)),
                  pl.BlockSpec((tm, TOP_K), lambda i: (i, 0))]
                 + [y_spec(k) for k in range(TOP_K)] * 2
                 + [full((d, 2 * D_SHARED)), full((D_SHARED, d)), full((1, d)), full((1, d))],
        out_specs=pl.BlockSpec((tm, d), lambda i: (i, 0)),
        out_shape=jax.ShapeDtypeStruct((t, d), F32),
        compiler_params=_cparams(("arbitrary",), 40),
        name="combine",
    )(h, w_t, *([yga] * TOP_K), *([ygb] * TOP_K), wsgu, wsd, g, b)


def _block_diag4(w):
    n = w.shape[0] // 4
    w4 = w.reshape(n, 4, LRU_BLOCK_DIM, LRU_BLOCK_DIM)
    eye = jnp.eye(4, dtype=w.dtype)
    return jnp.einsum('gaij,ab->gaibj', w4, eye).reshape(n, 4 * LRU_BLOCK_DIM, 4 * LRU_BLOCK_DIM)


def _layer(x, w_in, conv_w, conv_b, lru_w_a, lru_b_a, lru_w_x, lru_b_x, lru_lambda, w_lru_out,
           gla_w_gate_up, gla_b_gate, gla_norm_g, w_gla_out, w_mix_out, ln1_g, ln1_b,
           w_router, router_bias, w_exp_gate, w_exp_up, w_exp_down, w_sh_gate, w_sh_up, w_sh_down,
           ln2_g, ln2_b):
    batch, seq, d = x.shape
    t = batch * seq
    x2 = x.reshape(t, d)
    row = lambda v: v.reshape(1, -1)

    glow_lo = 2 * D_LRU + 2 * GLA_QK_DIM + 2 * GLA_V_DIM
    glow_hi = glow_lo + GLA_GATE_RANK
    w_p = jnp.concatenate(
        [w_in[:, :glow_lo], w_in[:, glow_hi:], w_in[:, glow_lo:glow_hi],
         jnp.zeros((d, GLOW_PAD - GLA_GATE_RANK), w_in.dtype)], axis=1).astype(BF16)
    p = _proj(x2, w_p, tm=min(512, t))

    wg = jnp.concatenate([_block_diag4(lru_w_a), _block_diag4(lru_w_x)], axis=2).astype(BF16)
    ts_lru = min(256, seq)
    lru_o = _lru(p, conv_w, row(conv_b), wg, row(lru_b_a), row(lru_b_x), row(lru_lambda),
                 batch, seq, ts_lru)

    wup = jnp.concatenate(
        [gla_w_gate_up, jnp.zeros((GLOW_PAD - GLA_GATE_RANK, GLA_QK_DIM), gla_w_gate_up.dtype)],
        axis=0).astype(BF16)
    tri = jnp.tril(jnp.ones((GLA_CHUNK, GLA_CHUNK), BF16))
    ts_gla = min(256, seq)
    gla_o = _gla(p, wup, row(gla_b_gate), row(gla_norm_g), tri, batch, seq, ts_gla)

    h, hpa, hpb = _merge(lru_o, gla_o, p, x2, w_lru_out.astype(BF16), w_gla_out.astype(BF16),
                         w_mix_out.astype(BF16), row(ln1_g), row(ln1_b), tm=min(512, t))

    tr = min(256, t)
    upper = jnp.triu(jnp.ones((tr, tr), F32), k=1).astype(BF16)
    wrt = w_router.T
    wrt_hi = wrt.astype(BF16)
    wrt_lo = (wrt - wrt_hi.astype(F32)).astype(BF16)
    idx, w, rank, cnt = _route(h, jnp.concatenate([wrt_hi, wrt_lo], axis=0),
                               router_bias.reshape(-1, 1), upper, tr)

    counts = cnt[:, 0].astype(I32)
    padded = (counts + EXPERT_ROWS - 1) // EXPERT_ROWS * EXPERT_ROWS
    pend = jnp.cumsum(padded)
    pstart = pend - padded
    dest = _slots(idx, rank, pstart.astype(F32).reshape(-1, 1), tl=min(512, t))
    n_blocks = (t * TOP_K) // EXPERT_ROWS + N_EXPERTS
    n_slots = n_blocks * EXPERT_ROWS
    blk = jnp.arange(n_blocks, dtype=I32)
    n_used = pend[-1] // EXPERT_ROWS
    bmap = jnp.minimum(blk, n_used - 1)
    bexp = jnp.minimum(
        jnp.sum((pend[None, :] <= (bmap * EXPERT_ROWS)[:, None]).astype(I32), axis=1),
        N_EXPERTS - 1)
    first = jnp.concatenate([jnp.ones((1,), I32), (bexp[1:] != bexp[:-1]).astype(I32)])
    flags = (blk < n_used).astype(I32) + 2 * first

    live_end = jnp.sum(jnp.where(bexp[:, None] == jnp.arange(N_EXPERTS, dtype=I32)[None, :],
                                 (pstart + counts)[None, :], 0), axis=1)
    nrow = jnp.clip(live_end - bmap * EXPERT_ROWS, 0, EXPERT_ROWS).astype(I32)

    dest_flat = dest.reshape(-1)
    xa = _sc_scatter_rows(hpa, dest_flat, n_slots)
    xb = _sc_scatter_rows(hpb, dest_flat, n_slots)
    ya, yb = _experts(bmap, bexp, flags, nrow, xa, xb, w_exp_gate, w_exp_up, w_exp_down)

    wsgu = jnp.concatenate([w_sh_gate, w_sh_up], axis=1).astype(BF16)
    out = _combine(h, w.T, _sc_gather_rows(ya, dest_flat), _sc_gather_rows(yb, dest_flat), wsgu,
                   w_sh_down.astype(BF16), row(ln2_g), row(ln2_b), tm=min(512, t))
    return out.reshape(batch, seq, d)


def kernel(x, w_in, conv_w, conv_b, lru_w_a, lru_b_a, lru_w_x, lru_b_x, lru_lambda, w_lru_out,
           gla_w_gate_up, gla_b_gate, gla_norm_g, w_gla_out, w_mix_out, ln1_g, ln1_b,
           w_router, router_bias, w_exp_gate, w_exp_up, w_exp_down, w_sh_gate, w_sh_up, w_sh_down,
           ln2_g, ln2_b):
    params = (w_in, conv_w, conv_b, lru_w_a, lru_b_a, lru_w_x, lru_b_x, lru_lambda, w_lru_out,
              gla_w_gate_up, gla_b_gate, gla_norm_g, w_gla_out, w_mix_out, ln1_g, ln1_b,
              w_router, router_bias, w_exp_gate, w_exp_up, w_exp_down, w_sh_gate, w_sh_up,
              w_sh_down, ln2_g, ln2_b)
    h = x
    for l in range(DEPTH):
        h = _layer(h, *(p[l] for p in params))
    return h
```

```python
import functools

import jax
import jax.numpy as jnp
from jax import lax
from jax.experimental import pallas as pl
from jax.experimental.pallas import tpu as pltpu
from jax.experimental.pallas import tpu_sc as plsc

F32 = jnp.float32
BF16 = jnp.bfloat16
I32 = jnp.int32
U32 = jnp.uint32

D_MODEL = 1024
D_LRU = D_MODEL
LRU_BLOCKS = 16
LRU_BLOCK_DIM = D_LRU // LRU_BLOCKS
CONV_WIDTH = 4
LRU_C = 8.0
GLA_HEADS = 4
GLA_QK_DIM = D_MODEL // 2
GLA_V_DIM = D_MODEL
GLA_DK = GLA_QK_DIM // GLA_HEADS
GLA_DV = GLA_V_DIM // GLA_HEADS
GLA_GATE_RANK = 16
GLA_TAU = 16.0
GLA_CHUNK = 64
N_EXPERTS = 256
TOP_K = 8
N_GROUPS = 8
GROUP_SIZE = N_EXPERTS // N_GROUPS
TOPK_GROUPS = 4
D_EXPERT = 256
D_SHARED = 256
ROUTED_SCALE = 2.5
DEPTH = 1
DN_ALPHA = (2.0 * DEPTH) ** 0.25
LN_EPS = 1e-5
RMS_EPS = 1e-6

LANES = 128
SUBLANES = 8
GLOW_PAD = LANES
N_MAIN = 2 * D_LRU + 2 * GLA_QK_DIM + 2 * GLA_V_DIM + 2 * D_MODEL
N_PROJ = N_MAIN + GLOW_PAD
PROJ_CHUNK = 512

EXPERT_ROWS = 512
SC_WINDOW = 128


def _cparams(semantics, vmem_mib):
    return pltpu.CompilerParams(dimension_semantics=semantics,
                                vmem_limit_bytes=vmem_mib * 1024 * 1024)


def _sigmoid(x):
    return jax.nn.sigmoid(x)


def _silu(x):
    return x * jax.nn.sigmoid(x)


def _gelu_tanh(x):
    c = 0.7978845608028654
    return x * (0.5 * (1.0 + jnp.tanh(c * (x + 0.044715 * (x * x * x)))))


def _pack_rows(v):
    n = v.shape[1] // 2
    lo = lax.bitcast_convert_type(v[:, :n].astype(BF16).astype(F32), U32)
    hi = lax.bitcast_convert_type(v[:, n:].astype(BF16).astype(F32), U32)
    return (lo >> 16) | (hi & jnp.uint32(0xFFFF0000))


def _unpack_rows(p):
    lo = lax.bitcast_convert_type(p << 16, F32)
    hi = lax.bitcast_convert_type(p & jnp.uint32(0xFFFF0000), F32)
    return lo, hi


def _layer_norm(z, g, b):
    mu = jnp.mean(z, axis=-1, keepdims=True)
    zc = z - mu
    var = jnp.mean(zc * zc, axis=-1, keepdims=True)
    return zc * lax.rsqrt(var + LN_EPS) * g + b


def _proj_kernel(x_ref, w_ref, o_ref):
    xb = x_ref[...].astype(BF16)
    for j in range(0, N_MAIN, PROJ_CHUNK):
        o_ref[:, j:j + PROJ_CHUNK] = jnp.dot(
            xb, w_ref[:, j:j + PROJ_CHUNK], preferred_element_type=F32).astype(BF16)
    o_ref[:, N_MAIN:] = jnp.dot(xb, w_ref[:, N_MAIN:], preferred_element_type=F32).astype(BF16)


def _proj(x2, w_p, tm):
    t, d = x2.shape
    return pl.pallas_call(
        _proj_kernel,
        grid=(t // tm,),
        in_specs=[pl.BlockSpec((tm, d), lambda i: (i, 0)),
                  pl.BlockSpec((d, N_PROJ), lambda i: (0, 0), pipeline_mode=pl.Buffered(1))],
        out_specs=pl.BlockSpec((tm, N_PROJ), lambda i: (i, 0)),
        out_shape=jax.ShapeDtypeStruct((t, N_PROJ), BF16),
        compiler_params=_cparams(("arbitrary",), 48),
        name="proj",
    )(x2, w_p)


def _lru_kernel(xa_ref, ga_ref, cw_ref, cb_ref, wg_ref, ba_ref, bx_ref, lam_ref, o_ref,
                xbuf, xc_s, a_s, u_s, cin_s, hnat, carry, *, ts):
    nb = ts // SUBLANES
    nslab = D_LRU // LANES
    grp = 4 * LRU_BLOCK_DIM
    hist = SUBLANES

    @pl.when(pl.program_id(1) == 0)
    def _init():
        for s in range(nslab):
            xbuf[s, 0:hist, :] = jnp.zeros((hist, LANES), F32)
        carry[...] = jnp.zeros((1, D_LRU), F32)

    for s in range(nslab):
        cs_ = slice(s * LANES, (s + 1) * LANES)
        xbuf[s, hist:, :] = xa_ref[:, cs_].astype(F32)
        taps = {}
        for r in range(SUBLANES):
            acc = cb_ref[:, cs_]
            for j in range(CONV_WIDTH):
                off = hist - (CONV_WIDTH - 1) + r + j
                if off not in taps:
                    taps[off] = xbuf[s, pl.ds(off, nb, stride=SUBLANES), :]
                acc = acc + cw_ref[j:j + 1, cs_] * taps[off]
            xc_s[r * nb:(r + 1) * nb, cs_] = acc
        xbuf[s, 0:hist, :] = xbuf[s, ts:ts + hist, :]

    lam = lam_ref[...]
    sp = jnp.maximum(-lam, 0.0) + jnp.log1p(jnp.exp(-jnp.abs(lam)))
    for g in range(D_LRU // grp):
        sl = slice(g * grp, (g + 1) * grp)
        xg = xc_s[:, sl]
        z = jnp.dot(xg.astype(BF16), wg_ref[g], preferred_element_type=F32)
        r = _sigmoid(z[:, :grp] + ba_ref[:, sl])
        i = _sigmoid(z[:, grp:] + bx_ref[:, sl])
        log_a = (-LRU_C * r) * sp[:, sl]
        a = jnp.exp(log_a)
        a_s[:, sl] = a
        u_s[:, sl] = jnp.sqrt(-jnp.tanh(log_a) * (a * a + 1.0)) * (i * xg)

    for r in range(1, SUBLANES):
        prev = slice((r - 1) * nb, r * nb)
        cur = slice(r * nb, (r + 1) * nb)
        a_r = a_s[cur, :]
        u_s[cur, :] = a_r * u_s[prev, :] + u_s[cur, :]
        a_s[cur, :] = a_r * a_s[prev, :]

    last = (SUBLANES - 1) * nb

    def _carry_step(b, c):
        cin_s[pl.ds(b, 1), :] = c
        return a_s[pl.ds(last + b, 1), :] * c + u_s[pl.ds(last + b, 1), :]

    carry[...] = lax.fori_loop(0, nb, _carry_step, carry[...])

    cin = cin_s[...]
    for r in range(SUBLANES):
        cur = slice(r * nb, (r + 1) * nb)
        h_r = u_s[cur, :] + a_s[cur, :] * cin
        for s in range(nslab):
            hnat[s, pl.ds(r, nb, stride=SUBLANES), :] = h_r[:, s * LANES:(s + 1) * LANES]

    for s in range(nslab):
        cs_ = slice(s * LANES, (s + 1) * LANES)
        o_ref[:, cs_] = (hnat[s] * _gelu_tanh(ga_ref[:, cs_].astype(F32))).astype(BF16)


def _lru(p, cw, cb, wg, ba, bx, lam, batch, seq, ts):
    t = batch * seq
    nst = seq // ts
    nslab = D_LRU // LANES
    row = lambda b, s: b * nst + s
    full = lambda shape: pl.BlockSpec(shape, lambda b, s: (0,) * len(shape))
    return pl.pallas_call(
        functools.partial(_lru_kernel, ts=ts),
        grid=(batch, nst),
        in_specs=[pl.BlockSpec((ts, D_LRU), lambda b, s: (row(b, s), 0)),
                  pl.BlockSpec((ts, D_LRU), lambda b, s: (row(b, s), 1)),
                  full((CONV_WIDTH, D_LRU)), full((1, D_LRU)),
                  full(wg.shape), full((1, D_LRU)), full((1, D_LRU)), full((1, D_LRU))],
        out_specs=pl.BlockSpec((ts, D_LRU), lambda b, s: (row(b, s), 0)),
        out_shape=jax.ShapeDtypeStruct((t, D_LRU), BF16),
        scratch_shapes=[pltpu.VMEM((nslab, ts + SUBLANES, LANES), F32),
                        pltpu.VMEM((ts, D_LRU), F32),
                        pltpu.VMEM((ts, D_LRU), F32),
                        pltpu.VMEM((ts, D_LRU), F32),
                        pltpu.VMEM((ts // SUBLANES, D_LRU), F32),
                        pltpu.VMEM((nslab, ts, LANES), F32),
                        pltpu.VMEM((1, D_LRU), F32)],
        compiler_params=_cparams(("arbitrary", "arbitrary"), 40),
        name="lru",
    )(p, p, cw, cb, wg, ba, bx, lam)


def _gla_kernel(q_ref, k_ref, v_ref, r_ref, gl_ref, wup_ref, bg_ref, ng_ref, tri_ref, o_ref,
                st_ref, *, ts):
    @pl.when(pl.program_id(1) == 0)
    def _init():
        st_ref[...] = jnp.zeros(st_ref.shape, F32)

    logit = jnp.dot(gl_ref[...], wup_ref[...], preferred_element_type=F32) + bg_ref[...]
    log_a = (jnp.minimum(logit, 0.0) - jnp.log1p(jnp.exp(-jnp.abs(logit)))) * (1.0 / GLA_TAU)
    tri = tri_ref[...]
    ri = lax.broadcasted_iota(I32, (GLA_CHUNK, GLA_CHUNK), 0)
    ci = lax.broadcasted_iota(I32, (GLA_CHUNK, GLA_CHUNK), 1)
    causal = ri >= ci
    nt = (((1,), (1,)), ((), ()))
    tn = (((0,), (0,)), ((), ()))
    for c in range(ts // GLA_CHUNK):
        rs = slice(c * GLA_CHUNK, (c + 1) * GLA_CHUNK)
        la = log_a[rs, :]
        la_hi = la.astype(BF16)
        la_r = la - la_hi.astype(F32)
        la_mid = la_r.astype(BF16)
        la_lo = (la_r - la_mid.astype(F32)).astype(BF16)
        parts = jnp.dot(tri, jnp.concatenate([la_hi, la_mid, la_lo], axis=1),
                        preferred_element_type=F32)
        bcum = (parts[:, :GLA_QK_DIM] + parts[:, GLA_QK_DIM:2 * GLA_QK_DIM]
                + parts[:, 2 * GLA_QK_DIM:])
        bl = bcum[GLA_CHUNK - 1:GLA_CHUNK, :]
        kf = k_ref[rs, :].astype(F32)
        qd = (q_ref[rs, :].astype(F32) * (GLA_DK ** -0.5)) * jnp.exp(bcum)
        ki = kf * jnp.exp(-bcum)
        ke = kf * jnp.exp(bl - bcum)
        gl = jnp.exp(bl)
        for h in range(GLA_HEADS):
            hs = slice(h * GLA_DK, (h + 1) * GLA_DK)
            vs = slice(h * GLA_DV, (h + 1) * GLA_DV)
            qd_h = qd[:, hs].astype(BF16)
            sc = lax.dot_general(qd_h, ki[:, hs].astype(BF16), nt, preferred_element_type=F32)
            sc = jnp.where(causal, sc, 0.0)
            v_h = v_ref[rs, vs]
            st_t = st_ref[h]
            o = jnp.dot(sc.astype(BF16), v_h, preferred_element_type=F32)
            o = o + lax.dot_general(qd_h, st_t.astype(BF16), nt, preferred_element_type=F32)
            st_ref[h] = st_t * gl[:, hs] + lax.dot_general(
                v_h, ke[:, hs].astype(BF16), tn, preferred_element_type=F32)
            ms = jnp.mean(o * o, axis=-1, keepdims=True)
            o = o * lax.rsqrt(ms + RMS_EPS) * ng_ref[:, vs]
            o_ref[rs, vs] = (o * _silu(r_ref[rs, vs].astype(F32))).astype(BF16)


def _gla(p, wup, bg, ng, tri, batch, seq, ts):
    t = batch * seq
    nst = seq // ts
    row = lambda b, s: b * nst + s
    full = lambda shape: pl.BlockSpec(shape, lambda b, s: (0,) * len(shape))
    q_blk = (2 * D_LRU) // GLA_QK_DIM
    v_blk = (2 * D_LRU + 2 * GLA_QK_DIM) // GLA_V_DIM
    gl_blk = N_MAIN // GLOW_PAD
    return pl.pallas_call(
        functools.partial(_gla_kernel, ts=ts),
        grid=(batch, nst),
        in_specs=[pl.BlockSpec((ts, GLA_QK_DIM), lambda b, s: (row(b, s), q_blk)),
                  pl.BlockSpec((ts, GLA_QK_DIM), lambda b, s: (row(b, s), q_blk + 1)),
                  pl.BlockSpec((ts, GLA_V_DIM), lambda b, s: (row(b, s), v_blk)),
                  pl.BlockSpec((ts, GLA_V_DIM), lambda b, s: (row(b, s), v_blk + 1)),
                  pl.BlockSpec((ts, GLOW_PAD), lambda b, s: (row(b, s), gl_blk)),
                  full((GLOW_PAD, GLA_QK_DIM)), full((1, GLA_QK_DIM)), full((1, GLA_V_DIM)),
                  full((GLA_CHUNK, GLA_CHUNK))],
        out_specs=pl.BlockSpec((ts, GLA_V_DIM), lambda b, s: (row(b, s), 0)),
        out_shape=jax.ShapeDtypeStruct((t, GLA_V_DIM), BF16),
        scratch_shapes=[pltpu.VMEM((GLA_HEADS, GLA_DV, GLA_DK), F32)],
        compiler_params=_cparams(("arbitrary", "arbitrary"), 40),
        name="gla",
    )(p, p, p, p, p, wup, bg, ng, tri)


def _merge_kernel(lru_ref, gla_ref, ga_ref, gb_ref, x_ref, wa_ref, wb_ref, wo_ref, g_ref, b_ref,
                  h_ref, hpa_ref, hpb_ref):
    ya = jnp.dot(lru_ref[...], wa_ref[...], preferred_element_type=F32)
    yb = jnp.dot(gla_ref[...], wb_ref[...], preferred_element_type=F32)
    merged = _sigmoid(ga_ref[...].astype(F32)) * ya + _sigmoid(gb_ref[...].astype(F32)) * yb
    mix = jnp.dot(merged.astype(BF16), wo_ref[...], preferred_element_type=F32)
    h = _layer_norm(DN_ALPHA * x_ref[...] + mix, g_ref[...], b_ref[...])
    h_ref[...] = h
    hp = _pack_rows(h)
    quarter = hp.shape[1] // 2
    hpa_ref[...] = hp[:, :quarter]
    hpb_ref[...] = hp[:, quarter:]


def _merge(lru_o, gla_o, p, x2, wa, wb, wo, g, b, tm):
    t, d = x2.shape
    ga_blk = (2 * D_LRU + 2 * GLA_QK_DIM + 2 * GLA_V_DIM) // D_MODEL
    rowb = lambda c: pl.BlockSpec((tm, d), lambda i: (i, c))
    full = lambda shape: pl.BlockSpec(shape, lambda i: (0,) * len(shape),
                                      pipeline_mode=pl.Buffered(1))
    return pl.pallas_call(
        _merge_kernel,
        grid=(t // tm,),
        in_specs=[rowb(0), rowb(0), rowb(ga_blk), rowb(ga_blk + 1), rowb(0),
                  full((d, d)), full((d, d)), full((d, d)), full((1, d)), full((1, d))],
        out_specs=[rowb(0)] + [pl.BlockSpec((tm, d // 4), lambda i: (i, 0))] * 2,
        out_shape=[jax.ShapeDtypeStruct((t, d), F32)]
                  + [jax.ShapeDtypeStruct((t, d // 4), U32)] * 2,
        compiler_params=_cparams(("arbitrary",), 48),
        name="merge",
    )(lru_o, gla_o, p, p, x2, wa, wb, wo, g, b)


def _route_kernel(h_ref, wrt_ref, bias_ref, upper_ref, idx_ref, w_ref, rank_ref, cnt_ref,
                  carry, *, tr):
    @pl.when(pl.program_id(0) == 0)
    def _init():
        carry[...] = jnp.zeros(carry.shape, F32)

    neg = -jnp.inf
    nt = (((1,), (1,)), ((), ()))
    h = h_ref[...]
    h_hi = h.astype(BF16)
    h_lo = (h - h_hi.astype(F32)).astype(BF16)
    both = lax.dot_general(wrt_ref[...], h_hi, nt, preferred_element_type=F32)
    logits = (both[:N_EXPERTS, :] + both[N_EXPERTS:, :]
              + lax.dot_general(wrt_ref[:N_EXPERTS, :], h_lo, nt, preferred_element_type=F32))
    scores = _sigmoid(logits)
    biased = scores + bias_ref[...]

    gs = []
    for g in range(N_GROUPS):
        blk = biased[g * GROUP_SIZE:(g + 1) * GROUP_SIZE, :]
        m1 = jnp.max(blk, axis=0, keepdims=True)
        eq = blk == m1
        n1 = jnp.sum(eq.astype(F32), axis=0, keepdims=True)
        m2 = jnp.max(jnp.where(eq, neg, blk), axis=0, keepdims=True)
        gs.append(m1 + jnp.where(n1 >= 2.0, m1, m2))
    sel = [jnp.zeros((1, tr), jnp.bool_) for _ in range(N_GROUPS)]
    for _ in range(TOPK_GROUPS):
        cur = [jnp.where(sel[g], neg, gs[g]) for g in range(N_GROUPS)]
        m = cur[0]
        for g in range(1, N_GROUPS):
            m = jnp.maximum(m, cur[g])
        found = jnp.zeros((1, tr), jnp.bool_)
        for g in range(N_GROUPS):
            pick = jnp.logical_and(cur[g] == m, jnp.logical_not(found))
            sel[g] = jnp.logical_or(sel[g], pick)
            found = jnp.logical_or(found, pick)
    cur = jnp.concatenate(
        [jnp.where(sel[g], biased[g * GROUP_SIZE:(g + 1) * GROUP_SIZE, :], neg)
         for g in range(N_GROUPS)], axis=0)

    rowid = lax.broadcasted_iota(I32, (N_EXPERTS, tr), 0)
    eligible = cur != neg
    idxs, ws = [], []
    for _ in range(TOP_K):
        m = jnp.max(cur, axis=0, keepdims=True)
        first = jnp.min(jnp.where(cur == m, rowid, N_EXPERTS), axis=0, keepdims=True)
        onehot = rowid == first
        ws.append(jnp.sum(jnp.where(onehot, scores, 0.0), axis=0, keepdims=True))
        idxs.append(first)
        cur = jnp.where(onehot, neg, cur)
    picked = jnp.logical_and(eligible, cur == neg)
    wsum = ws[0]
    for k in range(1, TOP_K):
        wsum = wsum + ws[k]

    pf = picked.astype(F32)
    before = jnp.dot(pf.astype(BF16), upper_ref[...], preferred_element_type=F32) + carry[...]
    for k in range(TOP_K):
        idx_ref[k:k + 1, :] = idxs[k]
        w_ref[k:k + 1, :] = ws[k] / wsum * ROUTED_SCALE
        rank_ref[k:k + 1, :] = jnp.sum(jnp.where(rowid == idxs[k], before, 0.0), axis=0,
                                       keepdims=True).astype(I32)
    carry[...] = carry[...] + jnp.sum(pf, axis=1, keepdims=True)
    cnt_ref[...] = carry[...]


def _route(h, wrt, bias_col, upper, tr):
    t, d = h.shape
    full = lambda shape: pl.BlockSpec(shape, lambda i: (0,) * len(shape))
    tok = lambda: pl.BlockSpec((TOP_K, tr), lambda i: (0, i))
    return pl.pallas_call(
        functools.partial(_route_kernel, tr=tr),
        grid=(t // tr,),
        in_specs=[pl.BlockSpec((tr, d), lambda i: (i, 0)),
                  full((2 * N_EXPERTS, d)), full((N_EXPERTS, 1)), full((tr, tr))],
        out_specs=[tok(), tok(), tok(), full((N_EXPERTS, 1))],
        out_shape=[jax.ShapeDtypeStruct((TOP_K, t), I32), jax.ShapeDtypeStruct((TOP_K, t), F32),
                   jax.ShapeDtypeStruct((TOP_K, t), I32),
                   jax.ShapeDtypeStruct((N_EXPERTS, 1), F32)],
        scratch_shapes=[pltpu.VMEM((N_EXPERTS, 1), F32)],
        compiler_params=_cparams(("arbitrary",), 32),
        name="route",
    )(h, wrt, bias_col, upper)


def _slots_kernel(idx_ref, rank_ref, pstart_ref, dest_ref, *, tl):
    rowid = lax.broadcasted_iota(I32, (N_EXPERTS, tl), 0)
    pstart = pstart_ref[...]
    for k in range(TOP_K):
        base = jnp.sum(jnp.where(rowid == idx_ref[k:k + 1, :], pstart, 0.0), axis=0, keepdims=True)
        dest_ref[k:k + 1, :] = base.astype(I32) + rank_ref[k:k + 1, :]


def _slots(idx, rank, pstart_col, tl):
    t = idx.shape[1]
    tok = lambda: pl.BlockSpec((TOP_K, tl), lambda i: (0, i))
    return pl.pallas_call(
        functools.partial(_slots_kernel, tl=tl),
        grid=(t // tl,),
        in_specs=[tok(), tok(), pl.BlockSpec((N_EXPERTS, 1), lambda i: (0, 0))],
        out_specs=tok(),
        out_shape=jax.ShapeDtypeStruct((TOP_K, t), I32),
        compiler_params=_cparams(("arbitrary",), 32),
        name="slots",
    )(idx, rank, pstart_col)


def _sc_scatter_rows(src, idx, n_out):
    nk, t = idx.shape
    dp = src.shape[1]
    mesh = plsc.VectorSubcoreMesh(core_axis_name="core", subcore_axis_name="subcore")

    @pl.kernel(out_type=jax.ShapeDtypeStruct((n_out, dp), src.dtype), mesh=mesh,
               name="sc_scatter")
    def _scatter(src_hbm, idx_hbm, out_hbm):
        def _body(src_vmem, idx_vmem):
            for k in range(nk):
                pltpu.sync_copy(src_vmem, out_hbm.at[idx_vmem.at[k]])

        pltpu.emit_pipeline(
            _body,
            grid=(t // SC_WINDOW,),
            in_specs=[pl.BlockSpec((SC_WINDOW, dp), lambda i: (i, 0)),
                      pl.BlockSpec((nk, SC_WINDOW), lambda i: (0, i))],
            out_specs=[],
            core_axis_name=("core", "subcore"),
            dimension_semantics=(pltpu.PARALLEL,),
        )(src_hbm, idx_hbm)

    return _scatter(src, idx)


def _expert_kernel(bmap_ref, bexp_ref, flag_ref, nrow_ref, xa_ref, xb_ref, wg_ref, wu_ref, wd_ref,
                   ya_ref, yb_ref, wgu_s, wd_s):
    i = pl.program_id(0)
    flags = flag_ref[i]
    half = D_MODEL // 2

    @pl.when(flags >= 2)
    def _cast():
        wgu_s[:, :D_EXPERT] = wg_ref[...].astype(BF16)
        wgu_s[:, D_EXPERT:] = wu_ref[...].astype(BF16)
        wd_s[...] = wd_ref[...].astype(BF16)

    @pl.when(flags % 2 == 1)
    def _compute():
        live = lax.broadcasted_iota(I32, xa_ref.shape, 0) < nrow_ref[i]
        lo_a, hi_a = _unpack_rows(jnp.where(live, xa_ref[...], jnp.uint32(0)))
        lo_b, hi_b = _unpack_rows(jnp.where(live, xb_ref[...], jnp.uint32(0)))
        x = jnp.concatenate([lo_a.astype(BF16), lo_b.astype(BF16), hi_a.astype(BF16),
                             hi_b.astype(BF16)], axis=1)
        z = jnp.dot(x, wgu_s[...], preferred_element_type=F32)
        act = _silu(z[:, :D_EXPERT]) * z[:, D_EXPERT:]
        yp = _pack_rows(jnp.dot(act.astype(BF16), wd_s[...], preferred_element_type=F32))
        ya_ref[...] = yp[:, :half // 2]
        yb_ref[...] = yp[:, half // 2:]

    @pl.when(flags % 2 == 0)
    def _tail():
        ya_ref[...] = jnp.zeros(ya_ref.shape, ya_ref.dtype)
        yb_ref[...] = jnp.zeros(yb_ref.shape, yb_ref.dtype)


def _experts(bmap, bexp, flags, nrow, xa, xb, wg, wu, wd):
    n_slots, dq = xa.shape
    d = 4 * dq
    nb = n_slots // EXPERT_ROWS
    grid_spec = pltpu.PrefetchScalarGridSpec(
        num_scalar_prefetch=4,
        grid=(nb,),
        in_specs=[pl.BlockSpec((EXPERT_ROWS, dq), lambda i, bm, be, fl, nr: (bm[i], 0)),
                  pl.BlockSpec((EXPERT_ROWS, dq), lambda i, bm, be, fl, nr: (bm[i], 0)),
                  pl.BlockSpec((None, d, D_EXPERT), lambda i, bm, be, fl, nr: (be[i], 0, 0)),
                  pl.BlockSpec((None, d, D_EXPERT), lambda i, bm, be, fl, nr: (be[i], 0, 0)),
                  pl.BlockSpec((None, D_EXPERT, d), lambda i, bm, be, fl, nr: (be[i], 0, 0))],
        out_specs=[pl.BlockSpec((EXPERT_ROWS, dq), lambda i, bm, be, fl, nr: (i, 0))] * 2,
        scratch_shapes=[pltpu.VMEM((d, 2 * D_EXPERT), BF16), pltpu.VMEM((D_EXPERT, d), BF16)],
    )
    return pl.pallas_call(
        _expert_kernel,
        grid_spec=grid_spec,
        out_shape=[jax.ShapeDtypeStruct((n_slots, dq), U32)] * 2,
        compiler_params=_cparams(("arbitrary",), 32),
        name="experts",
    )(bmap, bexp, flags, nrow, xa, xb, wg, wu, wd)


def _sc_gather_rows(table, idx):
    n = idx.shape[0]
    dp = table.shape[1]
    mesh = plsc.VectorSubcoreMesh(core_axis_name="core", subcore_axis_name="subcore")

    @pl.kernel(out_type=jax.ShapeDtypeStruct((n, dp), table.dtype), mesh=mesh, name="sc_gather")
    def _gather(table_hbm, idx_hbm, out_hbm):
        def _body(idx_vmem, out_vmem):
            pltpu.sync_copy(table_hbm.at[idx_vmem.at[0]], out_vmem)

        pltpu.emit_pipeline(
            _body,
            grid=(n // SC_WINDOW,),
            in_specs=[pl.BlockSpec((1, SC_WINDOW), lambda i: (0, i))],
            out_specs=[pl.BlockSpec((SC_WINDOW, dp), lambda i: (i, 0))],
            core_axis_name=("core", "subcore"),
            dimension_semantics=(pltpu.PARALLEL,),
        )(idx_hbm, out_hbm)

    return _gather(table, idx.reshape(1, n))


def _combine_kernel(h_ref, w_ref, *rest):
    ya_refs = rest[:TOP_K]
    yb_refs = rest[TOP_K:2 * TOP_K]
    wsgu_ref, wsd_ref, g_ref, b_ref, o_ref = rest[2 * TOP_K:]
    h = h_ref[...]
    z = jnp.dot(h.astype(BF16), wsgu_ref[...], preferred_element_type=F32)
    act = _silu(z[:, :D_SHARED]) * z[:, D_SHARED:]
    ffn = jnp.dot(act.astype(BF16), wsd_ref[...], preferred_element_type=F32)
    acc = [None] * 4
    for k in range(TOP_K):
        wk = w_ref[:, k:k + 1]
        parts = _unpack_rows(ya_refs[k][...]) + _unpack_rows(yb_refs[k][...])
        for j in range(4):
            acc[j] = parts[j] * wk if k == 0 else acc[j] + parts[j] * wk
    ffn = ffn + jnp.concatenate([acc[0], acc[2], acc[1], acc[3]], axis=1)
    o_ref[...] = _layer_norm(DN_ALPHA * h + ffn, g_ref[...], b_ref[...])


def _combine(h, w_t, yga, ygb, wsgu, wsd, g, b, tm):
    t, d = h.shape
    nt = t // tm
    full = lambda shape: pl.BlockSpec(shape, lambda i: (0,) * len(shape))
    y_spec = lambda k: pl.BlockSpec((tm, d // 4), lambda i: (k * nt + i, 0))
    return pl.pallas_call(
        _combine_kernel,
        grid=(nt,),
        in_specs=[pl.BlockSpec((tm, d), lambda i: (i, 0)),
                  pl.BlockSpec((tm, TOP_K), lambda i: (i, 0))]
                 + [y_spec(k) for k in range(TOP_K)] * 2
                 + [full((d, 2 * D_SHARED)), full((D_SHARED, d)), full((1, d)), full((1, d))],
        out_specs=pl.BlockSpec((tm, d), lambda i: (i, 0)),
        out_shape=jax.ShapeDtypeStruct((t, d), F32),
        compiler_params=_cparams(("arbitrary",), 40),
        name="combine",
    )(h, w_t, *([yga] * TOP_K), *([ygb] * TOP_K), wsgu, wsd, g, b)


def _block_diag4(w):
    n = w.shape[0] // 4
    w4 = w.reshape(n, 4, LRU_BLOCK_DIM, LRU_BLOCK_DIM)
    eye = jnp.eye(4, dtype=w.dtype)
    return jnp.einsum('gaij,ab->gaibj', w4, eye).reshape(n, 4 * LRU_BLOCK_DIM, 4 * LRU_BLOCK_DIM)


def _layer(x, w_in, conv_w, conv_b, lru_w_a, lru_b_a, lru_w_x, lru_b_x, lru_lambda, w_lru_out,
           gla_w_gate_up, gla_b_gate, gla_norm_g, w_gla_out, w_mix_out, ln1_g, ln1_b,
           w_router, router_bias, w_exp_gate, w_exp_up, w_exp_down, w_sh_gate, w_sh_up, w_sh_down,
           ln2_g, ln2_b):
    batch, seq, d = x.shape
    t = batch * seq
    x2 = x.reshape(t, d)
    row = lambda v: v.reshape(1, -1)

    glow_lo = 2 * D_LRU + 2 * GLA_QK_DIM + 2 * GLA_V_DIM
    glow_hi = glow_lo + GLA_GATE_RANK
    w_p = jnp.concatenate(
        [w_in[:, :glow_lo], w_in[:, glow_hi:], w_in[:, glow_lo:glow_hi],
         jnp.zeros((d, GLOW_PAD - GLA_GATE_RANK), w_in.dtype)], axis=1).astype(BF16)
    p = _proj(x2, w_p, tm=min(512, t))

    wg = jnp.concatenate([_block_diag4(lru_w_a), _block_diag4(lru_w_x)], axis=2).astype(BF16)
    ts_lru = min(512, seq)
    lru_o = _lru(p, conv_w, row(conv_b), wg, row(lru_b_a), row(lru_b_x), row(lru_lambda),
                 batch, seq, ts_lru)

    wup = jnp.concatenate(
        [gla_w_gate_up, jnp.zeros((GLOW_PAD - GLA_GATE_RANK, GLA_QK_DIM), gla_w_gate_up.dtype)],
        axis=0).astype(BF16)
    tri = jnp.tril(jnp.ones((GLA_CHUNK, GLA_CHUNK), BF16))
    ts_gla = min(256, seq)
    gla_o = _gla(p, wup, row(gla_b_gate), row(gla_norm_g), tri, batch, seq, ts_gla)

    h, hpa, hpb = _merge(lru_o, gla_o, p, x2, w_lru_out.astype(BF16), w_gla_out.astype(BF16),
                         w_mix_out.astype(BF16), row(ln1_g), row(ln1_b), tm=min(512, t))

    tr = min(256, t)
    upper = jnp.triu(jnp.ones((tr, tr), F32), k=1).astype(BF16)
    wrt = w_router.T
    wrt_top = lax.bitcast_convert_type(
        lax.bitcast_convert_type(wrt, U32) & jnp.uint32(0xFFFF0000), F32)
    wrt_hi = wrt_top.astype(BF16)
    wrt_lo = (wrt - wrt_top).astype(BF16)
    idx, w, rank, cnt = _route(h, jnp.concatenate([wrt_hi, wrt_lo], axis=0),
                               router_bias.reshape(-1, 1), upper, tr)

    counts = cnt[:, 0].astype(I32)
    padded = (counts + EXPERT_ROWS - 1) // EXPERT_ROWS * EXPERT_ROWS
    pend = jnp.cumsum(padded)
    pstart = pend - padded
    dest = _slots(idx, rank, pstart.astype(F32).reshape(-1, 1), tl=min(512, t))
    n_blocks = (t * TOP_K) // EXPERT_ROWS + N_EXPERTS
    n_slots = n_blocks * EXPERT_ROWS
    blk = jnp.arange(n_blocks, dtype=I32)
    n_used = pend[-1] // EXPERT_ROWS
    bmap = jnp.minimum(blk, n_used - 1)
    bexp = jnp.minimum(
        jnp.sum((pend[None, :] <= (bmap * EXPERT_ROWS)[:, None]).astype(I32), axis=1),
        N_EXPERTS - 1)
    first = jnp.concatenate([jnp.ones((1,), I32), (bexp[1:] != bexp[:-1]).astype(I32)])
    flags = (blk < n_used).astype(I32) + 2 * first

    live_end = jnp.sum(jnp.where(bexp[:, None] == jnp.arange(N_EXPERTS, dtype=I32)[None, :],
                                 (pstart + counts)[None, :], 0), axis=1)
    nrow = jnp.clip(live_end - bmap * EXPERT_ROWS, 0, EXPERT_ROWS).astype(I32)

    dest_flat = dest.reshape(-1)
    xa = _sc_scatter_rows(hpa, dest, n_slots)
    xb = _sc_scatter_rows(hpb, dest, n_slots)
    ya, yb = _experts(bmap, bexp, flags, nrow, xa, xb, w_exp_gate, w_exp_up, w_exp_down)

    wsgu = jnp.concatenate([w_sh_gate, w_sh_up], axis=1).astype(BF16)
    out = _combine(h, w.T, _sc_gather_rows(ya, dest_flat), _sc_gather_rows(yb, dest_flat), wsgu,
                   w_sh_down.astype(BF16), row(ln2_g), row(ln2_b), tm=min(512, t))
    return out.reshape(batch, seq, d)


def kernel(x, w_in, conv_w, conv_b, lru_w_a, lru_b_a, lru_w_x, lru_b_x, lru_lambda, w_lru_out,
           gla_w_gate_up, gla_b_gate, gla_norm_g, w_gla_out, w_mix_out, ln1_g, ln1_b,
           w_router, router_bias, w_exp_gate, w_exp_up, w_exp_down, w_sh_gate, w_sh_up, w_sh_down,
           ln2_g, ln2_b):
    params = (w_in, conv_w, conv_b, lru_w_a, lru_b_a, lru_w_x, lru_b_x, lru_lambda, w_lru_out,
              gla_w_gate_up, gla_b_gate, gla_norm_g, w_gla_out, w_mix_out, ln1_g, ln1_b,
              w_router, router_bias, w_exp_gate, w_exp_up, w_exp_down, w_sh_gate, w_sh_up,
              w_sh_down, ln2_g, ln2_b)
    h = x
    for l in range(DEPTH):
        h = _layer(h, *(p[l] for p in params))
    return h
```

```python
import functools

import jax
import jax.numpy as jnp
from jax import lax
from jax.experimental import pallas as pl
from jax.experimental.pallas import tpu as pltpu
from jax.experimental.pallas import tpu_sc as plsc

F32 = jnp.float32
BF16 = jnp.bfloat16
I32 = jnp.int32
U32 = jnp.uint32

D_MODEL = 1024
D_LRU = D_MODEL
LRU_BLOCKS = 16
LRU_BLOCK_DIM = D_LRU // LRU_BLOCKS
CONV_WIDTH = 4
LRU_C = 8.0
GLA_HEADS = 4
GLA_QK_DIM = D_MODEL // 2
GLA_V_DIM = D_MODEL
GLA_DK = GLA_QK_DIM // GLA_HEADS
GLA_DV = GLA_V_DIM // GLA_HEADS
GLA_GATE_RANK = 16
GLA_TAU = 16.0
GLA_CHUNK = 64
N_EXPERTS = 256
TOP_K = 8
N_GROUPS = 8
GROUP_SIZE = N_EXPERTS // N_GROUPS
TOPK_GROUPS = 4
D_EXPERT = 256
D_SHARED = 256
ROUTED_SCALE = 2.5
DEPTH = 1
DN_ALPHA = (2.0 * DEPTH) ** 0.25
LN_EPS = 1e-5
RMS_EPS = 1e-6

LANES = 128
SUBLANES = 8
GLOW_PAD = LANES
N_MAIN = 2 * D_LRU + 2 * GLA_QK_DIM + 2 * GLA_V_DIM + 2 * D_MODEL
N_PROJ = N_MAIN + GLOW_PAD
PROJ_CHUNK = 512

EXPERT_ROWS = 1024
EXPERT_SUB = 512
SC_WINDOW = 128


def _cparams(semantics, vmem_mib):
    return pltpu.CompilerParams(dimension_semantics=semantics,
                                vmem_limit_bytes=vmem_mib * 1024 * 1024)


def _sigmoid(x):
    return jax.nn.sigmoid(x)


def _silu(x):
    return x * jax.nn.sigmoid(x)


def _gelu_tanh(x):
    c = 0.7978845608028654
    return x * (0.5 * (1.0 + jnp.tanh(c * (x + 0.044715 * (x * x * x)))))


def _pack_rows(v):
    n = v.shape[1] // 2
    lo = lax.bitcast_convert_type(v[:, :n].astype(BF16).astype(F32), U32)
    hi = lax.bitcast_convert_type(v[:, n:].astype(BF16).astype(F32), U32)
    return (lo >> 16) | (hi & jnp.uint32(0xFFFF0000))


def _unpack_rows(p):
    lo = lax.bitcast_convert_type(p << 16, F32)
    hi = lax.bitcast_convert_type(p & jnp.uint32(0xFFFF0000), F32)
    return lo, hi


def _layer_norm(z, g, b):
    mu = jnp.mean(z, axis=-1, keepdims=True)
    zc = z - mu
    var = jnp.mean(zc * zc, axis=-1, keepdims=True)
    return zc * lax.rsqrt(var + LN_EPS) * g + b


def _proj_kernel(x_ref, w_ref, o_ref):
    xb = x_ref[...].astype(BF16)
    for j in range(0, N_MAIN, PROJ_CHUNK):
        o_ref[:, j:j + PROJ_CHUNK] = jnp.dot(
            xb, w_ref[:, j:j + PROJ_CHUNK], preferred_element_type=F32).astype(BF16)
    o_ref[:, N_MAIN:] = jnp.dot(xb, w_ref[:, N_MAIN:], preferred_element_type=F32).astype(BF16)


def _proj(x2, w_p, tm):
    t, d = x2.shape
    return pl.pallas_call(
        _proj_kernel,
        grid=(t // tm,),
        in_specs=[pl.BlockSpec((tm, d), lambda i: (i, 0)),
                  pl.BlockSpec((d, N_PROJ), lambda i: (0, 0), pipeline_mode=pl.Buffered(1))],
        out_specs=pl.BlockSpec((tm, N_PROJ), lambda i: (i, 0)),
        out_shape=jax.ShapeDtypeStruct((t, N_PROJ), BF16),
        compiler_params=_cparams(("arbitrary",), 48),
        name="proj",
    )(x2, w_p)


def _lru_kernel(xa_ref, ga_ref, cw_ref, cb_ref, wg_ref, ba_ref, bx_ref, lam_ref, o_ref,
                xbuf, xc_s, a_s, u_s, cin_s, hnat, carry, *, ts):
    nb = ts // SUBLANES
    nslab = D_LRU // LANES
    grp = 4 * LRU_BLOCK_DIM
    hist = SUBLANES

    @pl.when(pl.program_id(1) == 0)
    def _init():
        for s in range(nslab):
            xbuf[s, 0:hist, :] = jnp.zeros((hist, LANES), F32)
        carry[...] = jnp.zeros((1, D_LRU), F32)

    for s in range(nslab):
        cs_ = slice(s * LANES, (s + 1) * LANES)
        xbuf[s, hist:, :] = xa_ref[:, cs_].astype(F32)
        taps = {}
        for r in range(SUBLANES):
            acc = cb_ref[:, cs_]
            for j in range(CONV_WIDTH):
                off = hist - (CONV_WIDTH - 1) + r + j
                if off not in taps:
                    taps[off] = xbuf[s, pl.ds(off, nb, stride=SUBLANES), :]
                acc = acc + cw_ref[j:j + 1, cs_] * taps[off]
            xc_s[r * nb:(r + 1) * nb, cs_] = acc
        xbuf[s, 0:hist, :] = xbuf[s, ts:ts + hist, :]

    lam = lam_ref[...]
    sp = jnp.maximum(-lam, 0.0) + jnp.log1p(jnp.exp(-jnp.abs(lam)))
    for g in range(D_LRU // grp):
        sl = slice(g * grp, (g + 1) * grp)
        xg = xc_s[:, sl]
        z = jnp.dot(xg.astype(BF16), wg_ref[g], preferred_element_type=F32)
        r = _sigmoid(z[:, :grp] + ba_ref[:, sl])
        i = _sigmoid(z[:, grp:] + bx_ref[:, sl])
        log_a = (-LRU_C * r) * sp[:, sl]
        a = jnp.exp(log_a)
        a_s[:, sl] = a
        u_s[:, sl] = jnp.sqrt(-jnp.tanh(log_a) * (a * a + 1.0)) * (i * xg)

    for r in range(1, SUBLANES):
        prev = slice((r - 1) * nb, r * nb)
        cur = slice(r * nb, (r + 1) * nb)
        a_r = a_s[cur, :]
        u_s[cur, :] = a_r * u_s[prev, :] + u_s[cur, :]
        a_s[cur, :] = a_r * a_s[prev, :]

    last = (SUBLANES - 1) * nb

    def _carry_step(b, c):
        cin_s[pl.ds(b, 1), :] = c
        return a_s[pl.ds(last + b, 1), :] * c + u_s[pl.ds(last + b, 1), :]

    carry[...] = lax.fori_loop(0, nb, _carry_step, carry[...])

    cin = cin_s[...]
    for r in range(SUBLANES):
        cur = slice(r * nb, (r + 1) * nb)
        h_r = u_s[cur, :] + a_s[cur, :] * cin
        for s in range(nslab):
            hnat[s, pl.ds(r, nb, stride=SUBLANES), :] = h_r[:, s * LANES:(s + 1) * LANES]

    for s in range(nslab):
        cs_ = slice(s * LANES, (s + 1) * LANES)
        o_ref[:, cs_] = (hnat[s] * _gelu_tanh(ga_ref[:, cs_].astype(F32))).astype(BF16)


def _lru(p, cw, cb, wg, ba, bx, lam, batch, seq, ts):
    t = batch * seq
    nst = seq // ts
    nslab = D_LRU // LANES
    row = lambda b, s: b * nst + s
    full = lambda shape: pl.BlockSpec(shape, lambda b, s: (0,) * len(shape))
    return pl.pallas_call(
        functools.partial(_lru_kernel, ts=ts),
        grid=(batch, nst),
        in_specs=[pl.BlockSpec((ts, D_LRU), lambda b, s: (row(b, s), 0)),
                  pl.BlockSpec((ts, D_LRU), lambda b, s: (row(b, s), 1)),
                  full((CONV_WIDTH, D_LRU)), full((1, D_LRU)),
                  full(wg.shape), full((1, D_LRU)), full((1, D_LRU)), full((1, D_LRU))],
        out_specs=pl.BlockSpec((ts, D_LRU), lambda b, s: (row(b, s), 0)),
        out_shape=jax.ShapeDtypeStruct((t, D_LRU), BF16),
        scratch_shapes=[pltpu.VMEM((nslab, ts + SUBLANES, LANES), F32),
                        pltpu.VMEM((ts, D_LRU), F32),
                        pltpu.VMEM((ts, D_LRU), F32),
                        pltpu.VMEM((ts, D_LRU), F32),
                        pltpu.VMEM((ts // SUBLANES, D_LRU), F32),
                        pltpu.VMEM((nslab, ts, LANES), F32),
                        pltpu.VMEM((1, D_LRU), F32)],
        compiler_params=_cparams(("arbitrary", "arbitrary"), 40),
        name="lru",
    )(p, p, cw, cb, wg, ba, bx, lam)


def _gla_kernel(q_ref, k_ref, v_ref, r_ref, gl_ref, wup_ref, bg_ref, ng_ref, tri_ref, o_ref,
                st_ref, *, ts):
    @pl.when(pl.program_id(1) == 0)
    def _init():
        st_ref[...] = jnp.zeros(st_ref.shape, F32)

    logit = jnp.dot(gl_ref[...], wup_ref[...], preferred_element_type=F32) + bg_ref[...]
    log_a = (jnp.minimum(logit, 0.0) - jnp.log1p(jnp.exp(-jnp.abs(logit)))) * (1.0 / GLA_TAU)
    tri = tri_ref[...]
    ri = lax.broadcasted_iota(I32, (GLA_CHUNK, GLA_CHUNK), 0)
    ci = lax.broadcasted_iota(I32, (GLA_CHUNK, GLA_CHUNK), 1)
    causal = ri >= ci
    nt = (((1,), (1,)), ((), ()))
    tn = (((0,), (0,)), ((), ()))
    for c in range(ts // GLA_CHUNK):
        rs = slice(c * GLA_CHUNK, (c + 1) * GLA_CHUNK)
        la = log_a[rs, :]
        la_hi = la.astype(BF16)
        la_r = la - la_hi.astype(F32)
        la_mid = la_r.astype(BF16)
        la_lo = (la_r - la_mid.astype(F32)).astype(BF16)
        parts = jnp.dot(tri, jnp.concatenate([la_hi, la_mid, la_lo], axis=1),
                        preferred_element_type=F32)
        bcum = (parts[:, :GLA_QK_DIM] + parts[:, GLA_QK_DIM:2 * GLA_QK_DIM]
                + parts[:, 2 * GLA_QK_DIM:])
        bl = bcum[GLA_CHUNK - 1:GLA_CHUNK, :]
        kf = k_ref[rs, :].astype(F32)
        qd = (q_ref[rs, :].astype(F32) * (GLA_DK ** -0.5)) * jnp.exp(bcum)
        ki = kf * jnp.exp(-bcum)
        ke = kf * jnp.exp(bl - bcum)
        gl = jnp.exp(bl)
        for h in range(GLA_HEADS):
            hs = slice(h * GLA_DK, (h + 1) * GLA_DK)
            vs = slice(h * GLA_DV, (h + 1) * GLA_DV)
            qd_h = qd[:, hs].astype(BF16)
            sc = lax.dot_general(qd_h, ki[:, hs].astype(BF16), nt, preferred_element_type=F32)
            sc = jnp.where(causal, sc, 0.0)
            v_h = v_ref[rs, vs]
            st_t = st_ref[h]
            o = jnp.dot(sc.astype(BF16), v_h, preferred_element_type=F32)
            o = o + lax.dot_general(qd_h, st_t.astype(BF16), nt, preferred_element_type=F32)
            st_ref[h] = st_t * gl[:, hs] + lax.dot_general(
                v_h, ke[:, hs].astype(BF16), tn, preferred_element_type=F32)
            ms = jnp.mean(o * o, axis=-1, keepdims=True)
            o = o * lax.rsqrt(ms + RMS_EPS) * ng_ref[:, vs]
            o_ref[rs, vs] = (o * _silu(r_ref[rs, vs].astype(F32))).astype(BF16)


def _gla(p, wup, bg, ng, tri, batch, seq, ts):
    t = batch * seq
    nst = seq // ts
    row = lambda b, s: b * nst + s
    full = lambda shape: pl.BlockSpec(shape, lambda b, s: (0,) * len(shape))
    q_blk = (2 * D_LRU) // GLA_QK_DIM
    v_blk = (2 * D_LRU + 2 * GLA_QK_DIM) // GLA_V_DIM
    gl_blk = N_MAIN // GLOW_PAD
    return pl.pallas_call(
        functools.partial(_gla_kernel, ts=ts),
        grid=(batch, nst),
        in_specs=[pl.BlockSpec((ts, GLA_QK_DIM), lambda b, s: (row(b, s), q_blk)),
                  pl.BlockSpec((ts, GLA_QK_DIM), lambda b, s: (row(b, s), q_blk + 1)),
                  pl.BlockSpec((ts, GLA_V_DIM), lambda b, s: (row(b, s), v_blk)),
                  pl.BlockSpec((ts, GLA_V_DIM), lambda b, s: (row(b, s), v_blk + 1)),
                  pl.BlockSpec((ts, GLOW_PAD), lambda b, s: (row(b, s), gl_blk)),
                  full((GLOW_PAD, GLA_QK_DIM)), full((1, GLA_QK_DIM)), full((1, GLA_V_DIM)),
                  full((GLA_CHUNK, GLA_CHUNK))],
        out_specs=pl.BlockSpec((ts, GLA_V_DIM), lambda b, s: (row(b, s), 0)),
        out_shape=jax.ShapeDtypeStruct((t, GLA_V_DIM), BF16),
        scratch_shapes=[pltpu.VMEM((GLA_HEADS, GLA_DV, GLA_DK), F32)],
        compiler_params=_cparams(("arbitrary", "arbitrary"), 40),
        name="gla",
    )(p, p, p, p, p, wup, bg, ng, tri)


def _merge_kernel(lru_ref, gla_ref, ga_ref, gb_ref, x_ref, wa_ref, wb_ref, wo_ref, g_ref, b_ref,
                  h_ref, hpa_ref, hpb_ref):
    ya = jnp.dot(lru_ref[...], wa_ref[...], preferred_element_type=F32)
    yb = jnp.dot(gla_ref[...], wb_ref[...], preferred_element_type=F32)
    merged = _sigmoid(ga_ref[...].astype(F32)) * ya + _sigmoid(gb_ref[...].astype(F32)) * yb
    mix = jnp.dot(merged.astype(BF16), wo_ref[...], preferred_element_type=F32)
    h = _layer_norm(DN_ALPHA * x_ref[...] + mix, g_ref[...], b_ref[...])
    h_ref[...] = h
    hp = _pack_rows(h)
    quarter = hp.shape[1] // 2
    hpa_ref[...] = hp[:, :quarter]
    hpb_ref[...] = hp[:, quarter:]


def _merge(lru_o, gla_o, p, x2, wa, wb, wo, g, b, tm):
    t, d = x2.shape
    ga_blk = (2 * D_LRU + 2 * GLA_QK_DIM + 2 * GLA_V_DIM) // D_MODEL
    rowb = lambda c: pl.BlockSpec((tm, d), lambda i: (i, c))
    full = lambda shape: pl.BlockSpec(shape, lambda i: (0,) * len(shape),
                                      pipeline_mode=pl.Buffered(1))
    return pl.pallas_call(
        _merge_kernel,
        grid=(t // tm,),
        in_specs=[rowb(0), rowb(0), rowb(ga_blk), rowb(ga_blk + 1), rowb(0),
                  full((d, d)), full((d, d)), full((d, d)), full((1, d)), full((1, d))],
        out_specs=[rowb(0)] + [pl.BlockSpec((tm, d // 4), lambda i: (i, 0))] * 2,
        out_shape=[jax.ShapeDtypeStruct((t, d), F32)]
                  + [jax.ShapeDtypeStruct((t, d // 4), U32)] * 2,
        compiler_params=_cparams(("arbitrary",), 48),
        name="merge",
    )(lru_o, gla_o, p, p, x2, wa, wb, wo, g, b)


def _route_kernel(h_ref, wrt_ref, bias_ref, upper_ref, idx_ref, w_ref, rank_ref, cnt_ref,
                  carry, *, tr):
    @pl.when(pl.program_id(0) == 0)
    def _init():
        carry[...] = jnp.zeros(carry.shape, F32)

    neg = -jnp.inf
    nt = (((1,), (1,)), ((), ()))
    h = h_ref[...]
    h_hi = h.astype(BF16)
    h_lo = (h - h_hi.astype(F32)).astype(BF16)
    both = lax.dot_general(wrt_ref[...], h_hi, nt, preferred_element_type=F32)
    logits = (both[:N_EXPERTS, :] + both[N_EXPERTS:, :]
              + lax.dot_general(wrt_ref[:N_EXPERTS, :], h_lo, nt, preferred_element_type=F32))
    scores = _sigmoid(logits)
    biased = scores + bias_ref[...]

    gs = []
    for g in range(N_GROUPS):
        blk = biased[g * GROUP_SIZE:(g + 1) * GROUP_SIZE, :]
        m1 = jnp.max(blk, axis=0, keepdims=True)
        eq = blk == m1
        n1 = jnp.sum(eq.astype(F32), axis=0, keepdims=True)
        m2 = jnp.max(jnp.where(eq, neg, blk), axis=0, keepdims=True)
        gs.append(m1 + jnp.where(n1 >= 2.0, m1, m2))
    sel = [jnp.zeros((1, tr), jnp.bool_) for _ in range(N_GROUPS)]
    for _ in range(TOPK_GROUPS):
        cur = [jnp.where(sel[g], neg, gs[g]) for g in range(N_GROUPS)]
        m = cur[0]
        for g in range(1, N_GROUPS):
            m = jnp.maximum(m, cur[g])
        found = jnp.zeros((1, tr), jnp.bool_)
        for g in range(N_GROUPS):
            pick = jnp.logical_and(cur[g] == m, jnp.logical_not(found))
            sel[g] = jnp.logical_or(sel[g], pick)
            found = jnp.logical_or(found, pick)
    cur = jnp.concatenate(
        [jnp.where(sel[g], biased[g * GROUP_SIZE:(g + 1) * GROUP_SIZE, :], neg)
         for g in range(N_GROUPS)], axis=0)

    rowid = lax.broadcasted_iota(I32, (N_EXPERTS, tr), 0)
    eligible = cur != neg
    idxs, ws = [], []
    for _ in range(TOP_K):
        m = jnp.max(cur, axis=0, keepdims=True)
        first = jnp.min(jnp.where(cur == m, rowid, N_EXPERTS), axis=0, keepdims=True)
        onehot = rowid == first
        ws.append(jnp.sum(jnp.where(onehot, scores, 0.0), axis=0, keepdims=True))
        idxs.append(first)
        cur = jnp.where(onehot, neg, cur)
    picked = jnp.logical_and(eligible, cur == neg)
    wsum = ws[0]
    for k in range(1, TOP_K):
        wsum = wsum + ws[k]

    pf = picked.astype(F32)
    before = jnp.dot(pf.astype(BF16), upper_ref[...], preferred_element_type=F32) + carry[...]
    for k in range(TOP_K):
        idx_ref[k:k + 1, :] = idxs[k]
        w_ref[k:k + 1, :] = ws[k] / wsum * ROUTED_SCALE
        rank_ref[k:k + 1, :] = jnp.sum(jnp.where(rowid == idxs[k], before, 0.0), axis=0,
                                       keepdims=True).astype(I32)
    carry[...] = carry[...] + jnp.sum(pf, axis=1, keepdims=True)
    cnt_ref[...] = carry[...]


def _route(h, wrt, bias_col, upper, tr):
    t, d = h.shape
    full = lambda shape: pl.BlockSpec(shape, lambda i: (0,) * len(shape))
    tok = lambda: pl.BlockSpec((TOP_K, tr), lambda i: (0, i))
    return pl.pallas_call(
        functools.partial(_route_kernel, tr=tr),
        grid=(t // tr,),
        in_specs=[pl.BlockSpec((tr, d), lambda i: (i, 0)),
                  full((2 * N_EXPERTS, d)), full((N_EXPERTS, 1)), full((tr, tr))],
        out_specs=[tok(), tok(), tok(), full((N_EXPERTS, 1))],
        out_shape=[jax.ShapeDtypeStruct((TOP_K, t), I32), jax.ShapeDtypeStruct((TOP_K, t), F32),
                   jax.ShapeDtypeStruct((TOP_K, t), I32),
                   jax.ShapeDtypeStruct((N_EXPERTS, 1), F32)],
        scratch_shapes=[pltpu.VMEM((N_EXPERTS, 1), F32)],
        compiler_params=_cparams(("arbitrary",), 32),
        name="route",
    )(h, wrt, bias_col, upper)


def _slots_kernel(idx_ref, rank_ref, pstart_ref, dest_ref, *, tl):
    rowid = lax.broadcasted_iota(I32, (N_EXPERTS, tl), 0)
    pstart = pstart_ref[...]
    for k in range(TOP_K):
        base = jnp.sum(jnp.where(rowid == idx_ref[k:k + 1, :], pstart, 0.0), axis=0, keepdims=True)
        dest_ref[k:k + 1, :] = base.astype(I32) + rank_ref[k:k + 1, :]


def _slots(idx, rank, pstart_col, tl):
    t = idx.shape[1]
    tok = lambda: pl.BlockSpec((TOP_K, tl), lambda i: (0, i))
    return pl.pallas_call(
        functools.partial(_slots_kernel, tl=tl),
        grid=(t // tl,),
        in_specs=[tok(), tok(), pl.BlockSpec((N_EXPERTS, 1), lambda i: (0, 0))],
        out_specs=tok(),
        out_shape=jax.ShapeDtypeStruct((TOP_K, t), I32),
        compiler_params=_cparams(("arbitrary",), 32),
        name="slots",
    )(idx, rank, pstart_col)


def _sc_scatter_rows(src, idx, n_out):
    nk, t = idx.shape
    dp = src.shape[1]
    mesh = plsc.VectorSubcoreMesh(core_axis_name="core", subcore_axis_name="subcore")

    @pl.kernel(out_type=jax.ShapeDtypeStruct((n_out, dp), src.dtype), mesh=mesh,
               name="sc_scatter")
    def _scatter(src_hbm, idx_hbm, out_hbm):
        def _body(src_vmem, idx_vmem):
            for k in range(nk):
                pltpu.sync_copy(src_vmem, out_hbm.at[idx_vmem.at[k]])

        pltpu.emit_pipeline(
            _body,
            grid=(t // SC_WINDOW,),
            in_specs=[pl.BlockSpec((SC_WINDOW, dp), lambda i: (i, 0)),
                      pl.BlockSpec((nk, SC_WINDOW), lambda i: (0, i))],
            out_specs=[],
            core_axis_name=("core", "subcore"),
            dimension_semantics=(pltpu.PARALLEL,),
        )(src_hbm, idx_hbm)

    return _scatter(src, idx)


def _expert_kernel(bmap_ref, bexp_ref, flag_ref, nrow_ref, xa_ref, xb_ref, wg_ref, wu_ref, wd_ref,
                   ya_ref, yb_ref, wgu_s, wd_s):
    i = pl.program_id(0)
    flags = flag_ref[i]
    half = D_MODEL // 2

    @pl.when(flags >= 2)
    def _cast():
        wgu_s[:, :D_EXPERT] = wg_ref[...].astype(BF16)
        wgu_s[:, D_EXPERT:] = wu_ref[...].astype(BF16)
        wd_s[...] = wd_ref[...].astype(BF16)

    nrow = nrow_ref[i]
    for c in range(EXPERT_ROWS // EXPERT_SUB):
        rows = slice(c * EXPERT_SUB, (c + 1) * EXPERT_SUB)

        @pl.when(nrow > c * EXPERT_SUB)
        def _compute():
            rid = lax.broadcasted_iota(I32, (EXPERT_SUB, xa_ref.shape[1]), 0) + c * EXPERT_SUB
            live = rid < nrow
            lo_a, hi_a = _unpack_rows(jnp.where(live, xa_ref[rows, :], jnp.uint32(0)))
            lo_b, hi_b = _unpack_rows(jnp.where(live, xb_ref[rows, :], jnp.uint32(0)))
            x = jnp.concatenate([lo_a.astype(BF16), lo_b.astype(BF16), hi_a.astype(BF16),
                                 hi_b.astype(BF16)], axis=1)
            z = jnp.dot(x, wgu_s[...], preferred_element_type=F32)
            act = _silu(z[:, :D_EXPERT]) * z[:, D_EXPERT:]
            yp = _pack_rows(jnp.dot(act.astype(BF16), wd_s[...], preferred_element_type=F32))
            ya_ref[rows, :] = yp[:, :half // 2]
            yb_ref[rows, :] = yp[:, half // 2:]

        @pl.when(nrow <= c * EXPERT_SUB)
        def _empty():
            ya_ref[rows, :] = jnp.zeros((EXPERT_SUB, ya_ref.shape[1]), ya_ref.dtype)
            yb_ref[rows, :] = jnp.zeros((EXPERT_SUB, yb_ref.shape[1]), yb_ref.dtype)


def _experts(bmap, bexp, flags, nrow, xa, xb, wg, wu, wd):
    n_slots, dq = xa.shape
    d = 4 * dq
    nb = n_slots // EXPERT_ROWS
    grid_spec = pltpu.PrefetchScalarGridSpec(
        num_scalar_prefetch=4,
        grid=(nb,),
        in_specs=[pl.BlockSpec((EXPERT_ROWS, dq), lambda i, bm, be, fl, nr: (bm[i], 0)),
                  pl.BlockSpec((EXPERT_ROWS, dq), lambda i, bm, be, fl, nr: (bm[i], 0)),
                  pl.BlockSpec((None, d, D_EXPERT), lambda i, bm, be, fl, nr: (be[i], 0, 0)),
                  pl.BlockSpec((None, d, D_EXPERT), lambda i, bm, be, fl, nr: (be[i], 0, 0)),
                  pl.BlockSpec((None, D_EXPERT, d), lambda i, bm, be, fl, nr: (be[i], 0, 0))],
        out_specs=[pl.BlockSpec((EXPERT_ROWS, dq), lambda i, bm, be, fl, nr: (i, 0))] * 2,
        scratch_shapes=[pltpu.VMEM((d, 2 * D_EXPERT), BF16), pltpu.VMEM((D_EXPERT, d), BF16)],
    )
    return pl.pallas_call(
        _expert_kernel,
        grid_spec=grid_spec,
        out_shape=[jax.ShapeDtypeStruct((n_slots, dq), U32)] * 2,
        compiler_params=_cparams(("arbitrary",), 48),
        name="experts",
    )(bmap, bexp, flags, nrow, xa, xb, wg, wu, wd)


def _sc_gather_rows(table, idx):
    n = idx.shape[0]
    dp = table.shape[1]
    mesh = plsc.VectorSubcoreMesh(core_axis_name="core", subcore_axis_name="subcore")

    @pl.kernel(out_type=jax.ShapeDtypeStruct((n, dp), table.dtype), mesh=mesh, name="sc_gather")
    def _gather(table_hbm, idx_hbm, out_hbm):
        def _body(idx_vmem, out_vmem):
            pltpu.sync_copy(table_hbm.at[idx_vmem.at[0]], out_vmem)

        pltpu.emit_pipeline(
            _body,
            grid=(n // SC_WINDOW,),
            in_specs=[pl.BlockSpec((1, SC_WINDOW), lambda i: (0, i))],
            out_specs=[pl.BlockSpec((SC_WINDOW, dp), lambda i: (i, 0))],
            core_axis_name=("core", "subcore"),
            dimension_semantics=(pltpu.PARALLEL,),
        )(idx_hbm, out_hbm)

    return _gather(table, idx.reshape(1, n))


def _combine_kernel(h_ref, w_ref, *rest):
    ya_refs = rest[:TOP_K]
    yb_refs = rest[TOP_K:2 * TOP_K]
    wsgu_ref, wsd_ref, g_ref, b_ref, o_ref = rest[2 * TOP_K:]
    h = h_ref[...]
    z = jnp.dot(h.astype(BF16), wsgu_ref[...], preferred_element_type=F32)
    act = _silu(z[:, :D_SHARED]) * z[:, D_SHARED:]
    ffn = jnp.dot(act.astype(BF16), wsd_ref[...], preferred_element_type=F32)
    acc = [None] * 4
    for k in range(TOP_K):
        wk = w_ref[:, k:k + 1]
        parts = _unpack_rows(ya_refs[k][...]) + _unpack_rows(yb_refs[k][...])
        for j in range(4):
            acc[j] = parts[j] * wk if k == 0 else acc[j] + parts[j] * wk
    ffn = ffn + jnp.concatenate([acc[0], acc[2], acc[1], acc[3]], axis=1)
    o_ref[...] = _layer_norm(DN_ALPHA * h + ffn, g_ref[...], b_ref[...])


def _combine(h, w_t, yga, ygb, wsgu, wsd, g, b, tm):
    t, d = h.shape
    nt = t // tm
    full = lambda shape: pl.BlockSpec(shape, lambda i: (0,) * len(shape))
    y_spec = lambda k: pl.BlockSpec((tm, d // 4), lambda i: (k * nt + i, 0))
    return pl.pallas_call(
        _combine_kernel,
        grid=(nt,),
        in_specs=[pl.BlockSpec((tm, d), lambda i: (i, 0)),
                  pl.BlockSpec((tm, TOP_K), lambda i: (i, 0))]
                 + [y_spec(k) for k in range(TOP_K)] * 2
                 + [full((d, 2 * D_SHARED)), full((D_SHARED, d)), full((1, d)), full((1, d))],
        out_specs=pl.BlockSpec((tm, d), lambda i: (i, 0)),
        out_shape=jax.ShapeDtypeStruct((t, d), F32),
        compiler_params=_cparams(("arbitrary",), 40),
        name="combine",
    )(h, w_t, *([yga] * TOP_K), *([ygb] * TOP_K), wsgu, wsd, g, b)


def _block_diag4(w):
    n = w.shape[0] // 4
    w4 = w.reshape(n, 4, LRU_BLOCK_DIM, LRU_BLOCK_DIM)
    eye = jnp.eye(4, dtype=w.dtype)
    return jnp.einsum('gaij,ab->gaibj', w4, eye).reshape(n, 4 * LRU_BLOCK_DIM, 4 * LRU_BLOCK_DIM)


def _layer(x, w_in, conv_w, conv_b, lru_w_a, lru_b_a, lru_w_x, lru_b_x, lru_lambda, w_lru_out,
           gla_w_gate_up, gla_b_gate, gla_norm_g, w_gla_out, w_mix_out, ln1_g, ln1_b,
           w_router, router_bias, w_exp_gate, w_exp_up, w_exp_down, w_sh_gate, w_sh_up, w_sh_down,
           ln2_g, ln2_b):
    batch, seq, d = x.shape
    t = batch * seq
    x2 = x.reshape(t, d)
    row = lambda v: v.reshape(1, -1)

    glow_lo = 2 * D_LRU + 2 * GLA_QK_DIM + 2 * GLA_V_DIM
    glow_hi = glow_lo + GLA_GATE_RANK
    w_p = jnp.concatenate(
        [w_in[:, :glow_lo], w_in[:, glow_hi:], w_in[:, glow_lo:glow_hi],
         jnp.zeros((d, GLOW_PAD - GLA_GATE_RANK), w_in.dtype)], axis=1).astype(BF16)
    p = _proj(x2, w_p, tm=min(512, t))

    wg = jnp.concatenate([_block_diag4(lru_w_a), _block_diag4(lru_w_x)], axis=2).astype(BF16)
    ts_lru = min(512, seq)
    lru_o = _lru(p, conv_w, row(conv_b), wg, row(lru_b_a), row(lru_b_x), row(lru_lambda),
                 batch, seq, ts_lru)

    wup = jnp.concatenate(
        [gla_w_gate_up, jnp.zeros((GLOW_PAD - GLA_GATE_RANK, GLA_QK_DIM), gla_w_gate_up.dtype)],
        axis=0).astype(BF16)
    tri = jnp.tril(jnp.ones((GLA_CHUNK, GLA_CHUNK), BF16))
    ts_gla = min(512, seq)
    gla_o = _gla(p, wup, row(gla_b_gate), row(gla_norm_g), tri, batch, seq, ts_gla)

    h, hpa, hpb = _merge(lru_o, gla_o, p, x2, w_lru_out.astype(BF16), w_gla_out.astype(BF16),
                         w_mix_out.astype(BF16), row(ln1_g), row(ln1_b), tm=min(512, t))

    tr = min(256, t)
    upper = jnp.triu(jnp.ones((tr, tr), F32), k=1).astype(BF16)
    wrt = w_router.T
    wrt_top = lax.bitcast_convert_type(
        lax.bitcast_convert_type(wrt, U32) & jnp.uint32(0xFFFF0000), F32)
    wrt_hi = wrt_top.astype(BF16)
    wrt_lo = (wrt - wrt_top).astype(BF16)
    idx, w, rank, cnt = _route(h, jnp.concatenate([wrt_hi, wrt_lo], axis=0),
                               router_bias.reshape(-1, 1), upper, tr)

    counts = cnt[:, 0].astype(I32)
    padded = (counts + EXPERT_ROWS - 1) // EXPERT_ROWS * EXPERT_ROWS
    pend = jnp.cumsum(padded)
    pstart = pend - padded
    dest = _slots(idx, rank, pstart.astype(F32).reshape(-1, 1), tl=min(512, t))
    n_blocks = (t * TOP_K) // EXPERT_ROWS + N_EXPERTS
    n_slots = n_blocks * EXPERT_ROWS
    blk = jnp.arange(n_blocks, dtype=I32)
    n_used = pend[-1] // EXPERT_ROWS
    bmap = jnp.minimum(blk, n_used - 1)
    bexp = jnp.minimum(
        jnp.sum((pend[None, :] <= (bmap * EXPERT_ROWS)[:, None]).astype(I32), axis=1),
        N_EXPERTS - 1)
    first = jnp.concatenate([jnp.ones((1,), I32), (bexp[1:] != bexp[:-1]).astype(I32)])
    flags = (blk < n_used).astype(I32) + 2 * first

    live_end = jnp.sum(jnp.where(bexp[:, None] == jnp.arange(N_EXPERTS, dtype=I32)[None, :],
                                 (pstart + counts)[None, :], 0), axis=1)
    nrow = jnp.where(blk < n_used,
                     jnp.clip(live_end - bmap * EXPERT_ROWS, 0, EXPERT_ROWS), 0).astype(I32)

    dest_flat = dest.reshape(-1)
    xa = _sc_scatter_rows(hpa, dest, n_slots)
    xb = _sc_scatter_rows(hpb, dest, n_slots)
    ya, yb = _experts(bmap, bexp, flags, nrow, xa, xb, w_exp_gate, w_exp_up, w_exp_down)

    wsgu = jnp.concatenate([w_sh_gate, w_sh_up], axis=1).astype(BF16)
    out = _combine(h, w.T, _sc_gather_rows(ya, dest_flat), _sc_gather_rows(yb, dest_flat), wsgu,
                   w_sh_down.astype(BF16), row(ln2_g), row(ln2_b), tm=min(512, t))
    return out.reshape(batch, seq, d)


def kernel(x, w_in, conv_w, conv_b, lru_w_a, lru_b_a, lru_w_x, lru_b_x, lru_lambda, w_lru_out,
           gla_w_gate_up, gla_b_gate, gla_norm_g, w_gla_out, w_mix_out, ln1_g, ln1_b,
           w_router, router_bias, w_exp_gate, w_exp_up, w_exp_down, w_sh_gate, w_sh_up, w_sh_down,
           ln2_g, ln2_b):
    params = (w_in, conv_w, conv_b, lru_w_a, lru_b_a, lru_w_x, lru_b_x, lru_lambda, w_lru_out,
              gla_w_gate_up, gla_b_gate, gla_norm_g, w_gla_out, w_mix_out, ln1_g, ln1_b,
              w_router, router_bias, w_exp_gate, w_exp_up, w_exp_down, w_sh_gate, w_sh_up,
              w_sh_down, ln2_g, ln2_b)
    h = x
    for l in range(DEPTH):
        h = _layer(h, *(p[l] for p in params))
    return h
```

```python
import functools

import jax
import jax.numpy as jnp
from jax import lax
from jax.experimental import pallas as pl
from jax.experimental.pallas import tpu as pltpu
from jax.experimental.pallas import tpu_sc as plsc

F32 = jnp.float32
BF16 = jnp.bfloat16
I32 = jnp.int32
U32 = jnp.uint32

D_MODEL = 1024
D_LRU = D_MODEL
LRU_BLOCKS = 16
LRU_BLOCK_DIM = D_LRU // LRU_BLOCKS
CONV_WIDTH = 4
LRU_C = 8.0
GLA_HEADS = 4
GLA_QK_DIM = D_MODEL // 2
GLA_V_DIM = D_MODEL
GLA_DK = GLA_QK_DIM // GLA_HEADS
GLA_DV = GLA_V_DIM // GLA_HEADS
GLA_GATE_RANK = 16
GLA_TAU = 16.0
GLA_CHUNK = 64
N_EXPERTS = 256
TOP_K = 8
N_GROUPS = 8
GROUP_SIZE = N_EXPERTS // N_GROUPS
TOPK_GROUPS = 4
D_EXPERT = 256
D_SHARED = 256
ROUTED_SCALE = 2.5
DEPTH = 1
DN_ALPHA = (2.0 * DEPTH) ** 0.25
LN_EPS = 1e-5
RMS_EPS = 1e-6

LANES = 128
SUBLANES = 8
GLOW_PAD = LANES
N_MAIN = 2 * D_LRU + 2 * GLA_QK_DIM + 2 * GLA_V_DIM + 2 * D_MODEL
N_PROJ = N_MAIN + GLOW_PAD
PROJ_CHUNK = 512

EXPERT_ROWS = 1024
EXPERT_SUB = 512
SC_WINDOW = 128
COMBINE_CHUNKS = 4


def _cparams(semantics, vmem_mib):
    return pltpu.CompilerParams(dimension_semantics=semantics,
                                vmem_limit_bytes=vmem_mib * 1024 * 1024)


def _sigmoid(x):
    return jax.nn.sigmoid(x)


def _silu(x):
    return x * jax.nn.sigmoid(x)


def _gelu_tanh(x):
    c = 0.7978845608028654
    return x * (0.5 * (1.0 + jnp.tanh(c * (x + 0.044715 * (x * x * x)))))


def _pack_rows(v):
    n = v.shape[1] // 2
    lo = lax.bitcast_convert_type(v[:, :n].astype(BF16).astype(F32), U32)
    hi = lax.bitcast_convert_type(v[:, n:].astype(BF16).astype(F32), U32)
    return (lo >> 16) | (hi & jnp.uint32(0xFFFF0000))


def _unpack_rows(p):
    lo = lax.bitcast_convert_type(p << 16, F32)
    hi = lax.bitcast_convert_type(p & jnp.uint32(0xFFFF0000), F32)
    return lo, hi


def _layer_norm(z, g, b):
    mu = jnp.mean(z, axis=-1, keepdims=True)
    zc = z - mu
    var = jnp.mean(zc * zc, axis=-1, keepdims=True)
    return zc * lax.rsqrt(var + LN_EPS) * g + b


def _proj_kernel(x_ref, w_ref, o_ref):
    xb = x_ref[...].astype(BF16)
    for j in range(0, N_MAIN, PROJ_CHUNK):
        o_ref[:, j:j + PROJ_CHUNK] = jnp.dot(
            xb, w_ref[:, j:j + PROJ_CHUNK], preferred_element_type=F32).astype(BF16)
    o_ref[:, N_MAIN:] = jnp.dot(xb, w_ref[:, N_MAIN:], preferred_element_type=F32).astype(BF16)


def _proj(x2, w_p, tm):
    t, d = x2.shape
    return pl.pallas_call(
        _proj_kernel,
        grid=(t // tm,),
        in_specs=[pl.BlockSpec((tm, d), lambda i: (i, 0)),
                  pl.BlockSpec((d, N_PROJ), lambda i: (0, 0), pipeline_mode=pl.Buffered(1))],
        out_specs=pl.BlockSpec((tm, N_PROJ), lambda i: (i, 0)),
        out_shape=jax.ShapeDtypeStruct((t, N_PROJ), BF16),
        compiler_params=_cparams(("arbitrary",), 48),
        name="proj",
    )(x2, w_p)


def _lru_kernel(xa_ref, ga_ref, cw_ref, cb_ref, wg_ref, ba_ref, bx_ref, lam_ref, o_ref,
                xbuf, xc_s, a_s, u_s, cin_s, hnat, carry, *, ts):
    nb = ts // SUBLANES
    nslab = D_LRU // LANES
    grp = 4 * LRU_BLOCK_DIM
    hist = SUBLANES

    @pl.when(pl.program_id(1) == 0)
    def _init():
        for s in range(nslab):
            xbuf[s, 0:hist, :] = jnp.zeros((hist, LANES), F32)
        carry[...] = jnp.zeros((1, D_LRU), F32)

    for s in range(nslab):
        cs_ = slice(s * LANES, (s + 1) * LANES)
        xbuf[s, hist:, :] = xa_ref[:, cs_].astype(F32)
        taps = {}
        for r in range(SUBLANES):
            acc = cb_ref[:, cs_]
            for j in range(CONV_WIDTH):
                off = hist - (CONV_WIDTH - 1) + r + j
                if off not in taps:
                    taps[off] = xbuf[s, pl.ds(off, nb, stride=SUBLANES), :]
                acc = acc + cw_ref[j:j + 1, cs_] * taps[off]
            xc_s[r * nb:(r + 1) * nb, cs_] = acc
        xbuf[s, 0:hist, :] = xbuf[s, ts:ts + hist, :]

    lam = lam_ref[...]
    sp = jnp.maximum(-lam, 0.0) + jnp.log1p(jnp.exp(-jnp.abs(lam)))
    for g in range(D_LRU // grp):
        sl = slice(g * grp, (g + 1) * grp)
        xg = xc_s[:, sl]
        z = jnp.dot(xg.astype(BF16), wg_ref[g], preferred_element_type=F32)
        r = _sigmoid(z[:, :grp] + ba_ref[:, sl])
        i = _sigmoid(z[:, grp:] + bx_ref[:, sl])
        log_a = (-LRU_C * r) * sp[:, sl]
        a = jnp.exp(log_a)
        a_s[:, sl] = a
        u_s[:, sl] = jnp.sqrt(-jnp.tanh(log_a) * (a * a + 1.0)) * (i * xg)

    for r in range(1, SUBLANES):
        prev = slice((r - 1) * nb, r * nb)
        cur = slice(r * nb, (r + 1) * nb)
        a_r = a_s[cur, :]
        u_s[cur, :] = a_r * u_s[prev, :] + u_s[cur, :]
        a_s[cur, :] = a_r * a_s[prev, :]

    last = (SUBLANES - 1) * nb

    def _carry_step(b, c):
        cin_s[pl.ds(b, 1), :] = c
        return a_s[pl.ds(last + b, 1), :] * c + u_s[pl.ds(last + b, 1), :]

    carry[...] = lax.fori_loop(0, nb, _carry_step, carry[...])

    cin = cin_s[...]
    for r in range(SUBLANES):
        cur = slice(r * nb, (r + 1) * nb)
        h_r = u_s[cur, :] + a_s[cur, :] * cin
        for s in range(nslab):
            hnat[s, pl.ds(r, nb, stride=SUBLANES), :] = h_r[:, s * LANES:(s + 1) * LANES]

    for s in range(nslab):
        cs_ = slice(s * LANES, (s + 1) * LANES)
        o_ref[:, cs_] = (hnat[s] * _gelu_tanh(ga_ref[:, cs_].astype(F32))).astype(BF16)


def _lru(p, cw, cb, wg, ba, bx, lam, batch, seq, ts):
    t = batch * seq
    nst = seq // ts
    nslab = D_LRU // LANES
    row = lambda b, s: b * nst + s
    full = lambda shape: pl.BlockSpec(shape, lambda b, s: (0,) * len(shape))
    return pl.pallas_call(
        functools.partial(_lru_kernel, ts=ts),
        grid=(batch, nst),
        in_specs=[pl.BlockSpec((ts, D_LRU), lambda b, s: (row(b, s), 0)),
                  pl.BlockSpec((ts, D_LRU), lambda b, s: (row(b, s), 1)),
                  full((CONV_WIDTH, D_LRU)), full((1, D_LRU)),
                  full(wg.shape), full((1, D_LRU)), full((1, D_LRU)), full((1, D_LRU))],
        out_specs=pl.BlockSpec((ts, D_LRU), lambda b, s: (row(b, s), 0)),
        out_shape=jax.ShapeDtypeStruct((t, D_LRU), BF16),
        scratch_shapes=[pltpu.VMEM((nslab, ts + SUBLANES, LANES), F32),
                        pltpu.VMEM((ts, D_LRU), F32),
                        pltpu.VMEM((ts, D_LRU), F32),
                        pltpu.VMEM((ts, D_LRU), F32),
                        pltpu.VMEM((ts // SUBLANES, D_LRU), F32),
                        pltpu.VMEM((nslab, ts, LANES), F32),
                        pltpu.VMEM((1, D_LRU), F32)],
        compiler_params=_cparams(("arbitrary", "arbitrary"), 40),
        name="lru",
    )(p, p, cw, cb, wg, ba, bx, lam)


def _gla_kernel(q_ref, k_ref, v_ref, r_ref, gl_ref, wup_ref, bg_ref, ng_ref, tri_ref, o_ref,
                st_ref, *, ts):
    @pl.when(pl.program_id(1) == 0)
    def _init():
        st_ref[...] = jnp.zeros(st_ref.shape, F32)

    logit = jnp.dot(gl_ref[...], wup_ref[...], preferred_element_type=F32) + bg_ref[...]
    log_a = (jnp.minimum(logit, 0.0) - jnp.log1p(jnp.exp(-jnp.abs(logit)))) * (1.0 / GLA_TAU)
    tri = tri_ref[...]
    ri = lax.broadcasted_iota(I32, (GLA_CHUNK, GLA_CHUNK), 0)
    ci = lax.broadcasted_iota(I32, (GLA_CHUNK, GLA_CHUNK), 1)
    causal = ri >= ci
    nt = (((1,), (1,)), ((), ()))
    tn = (((0,), (0,)), ((), ()))
    for c in range(ts // GLA_CHUNK):
        rs = slice(c * GLA_CHUNK, (c + 1) * GLA_CHUNK)
        la = log_a[rs, :]
        la_hi = la.astype(BF16)
        la_r = la - la_hi.astype(F32)
        la_mid = la_r.astype(BF16)
        la_lo = (la_r - la_mid.astype(F32)).astype(BF16)
        parts = jnp.dot(tri, jnp.concatenate([la_hi, la_mid, la_lo], axis=1),
                        preferred_element_type=F32)
        bcum = (parts[:, :GLA_QK_DIM] + parts[:, GLA_QK_DIM:2 * GLA_QK_DIM]
                + parts[:, 2 * GLA_QK_DIM:])
        bl = bcum[GLA_CHUNK - 1:GLA_CHUNK, :]
        kf = k_ref[rs, :].astype(F32)
        qd = (q_ref[rs, :].astype(F32) * (GLA_DK ** -0.5)) * jnp.exp(bcum)
        ki = kf * jnp.exp(-bcum)
        ke = kf * jnp.exp(bl - bcum)
        gl = jnp.exp(bl)
        for h in range(GLA_HEADS):
            hs = slice(h * GLA_DK, (h + 1) * GLA_DK)
            vs = slice(h * GLA_DV, (h + 1) * GLA_DV)
            qd_h = qd[:, hs].astype(BF16)
            sc = lax.dot_general(qd_h, ki[:, hs].astype(BF16), nt, preferred_element_type=F32)
            sc = jnp.where(causal, sc, 0.0)
            v_h = v_ref[rs, vs]
            st_t = st_ref[h]
            o = jnp.dot(sc.astype(BF16), v_h, preferred_element_type=F32)
            o = o + lax.dot_general(qd_h, st_t.astype(BF16), nt, preferred_element_type=F32)
            st_ref[h] = st_t * gl[:, hs] + lax.dot_general(
                v_h, ke[:, hs].astype(BF16), tn, preferred_element_type=F32)
            ms = jnp.mean(o * o, axis=-1, keepdims=True)
            o = o * lax.rsqrt(ms + RMS_EPS) * ng_ref[:, vs]
            o_ref[rs, vs] = (o * _silu(r_ref[rs, vs].astype(F32))).astype(BF16)


def _gla(p, wup, bg, ng, tri, batch, seq, ts):
    t = batch * seq
    nst = seq // ts
    row = lambda b, s: b * nst + s
    full = lambda shape: pl.BlockSpec(shape, lambda b, s: (0,) * len(shape))
    q_blk = (2 * D_LRU) // GLA_QK_DIM
    v_blk = (2 * D_LRU + 2 * GLA_QK_DIM) // GLA_V_DIM
    gl_blk = N_MAIN // GLOW_PAD
    return pl.pallas_call(
        functools.partial(_gla_kernel, ts=ts),
        grid=(batch, nst),
        in_specs=[pl.BlockSpec((ts, GLA_QK_DIM), lambda b, s: (row(b, s), q_blk)),
                  pl.BlockSpec((ts, GLA_QK_DIM), lambda b, s: (row(b, s), q_blk + 1)),
                  pl.BlockSpec((ts, GLA_V_DIM), lambda b, s: (row(b, s), v_blk)),
                  pl.BlockSpec((ts, GLA_V_DIM), lambda b, s: (row(b, s), v_blk + 1)),
                  pl.BlockSpec((ts, GLOW_PAD), lambda b, s: (row(b, s), gl_blk)),
                  full((GLOW_PAD, GLA_QK_DIM)), full((1, GLA_QK_DIM)), full((1, GLA_V_DIM)),
                  full((GLA_CHUNK, GLA_CHUNK))],
        out_specs=pl.BlockSpec((ts, GLA_V_DIM), lambda b, s: (row(b, s), 0)),
        out_shape=jax.ShapeDtypeStruct((t, GLA_V_DIM), BF16),
        scratch_shapes=[pltpu.VMEM((GLA_HEADS, GLA_DV, GLA_DK), F32)],
        compiler_params=_cparams(("arbitrary", "arbitrary"), 40),
        name="gla",
    )(p, p, p, p, p, wup, bg, ng, tri)


def _merge_kernel(lru_ref, gla_ref, ga_ref, gb_ref, x_ref, wa_ref, wb_ref, wo_ref, g_ref, b_ref,
                  h_ref, hpa_ref, hpb_ref):
    ya = jnp.dot(lru_ref[...], wa_ref[...], preferred_element_type=F32)
    yb = jnp.dot(gla_ref[...], wb_ref[...], preferred_element_type=F32)
    merged = _sigmoid(ga_ref[...].astype(F32)) * ya + _sigmoid(gb_ref[...].astype(F32)) * yb
    mix = jnp.dot(merged.astype(BF16), wo_ref[...], preferred_element_type=F32)
    h = _layer_norm(DN_ALPHA * x_ref[...] + mix, g_ref[...], b_ref[...])
    h_ref[...] = h
    hp = _pack_rows(h)
    quarter = hp.shape[1] // 2
    hpa_ref[...] = hp[:, :quarter]
    hpb_ref[...] = hp[:, quarter:]


def _merge(lru_o, gla_o, p, x2, wa, wb, wo, g, b, tm):
    t, d = x2.shape
    ga_blk = (2 * D_LRU + 2 * GLA_QK_DIM + 2 * GLA_V_DIM) // D_MODEL
    rowb = lambda c: pl.BlockSpec((tm, d), lambda i: (i, c))
    full = lambda shape: pl.BlockSpec(shape, lambda i: (0,) * len(shape),
                                      pipeline_mode=pl.Buffered(1))
    return pl.pallas_call(
        _merge_kernel,
        grid=(t // tm,),
        in_specs=[rowb(0), rowb(0), rowb(ga_blk), rowb(ga_blk + 1), rowb(0),
                  full((d, d)), full((d, d)), full((d, d)), full((1, d)), full((1, d))],
        out_specs=[rowb(0)] + [pl.BlockSpec((tm, d // 4), lambda i: (i, 0))] * 2,
        out_shape=[jax.ShapeDtypeStruct((t, d), F32)]
                  + [jax.ShapeDtypeStruct((t, d // 4), U32)] * 2,
        compiler_params=_cparams(("arbitrary",), 48),
        name="merge",
    )(lru_o, gla_o, p, p, x2, wa, wb, wo, g, b)


def _route_kernel(h_ref, wrt_ref, bias_ref, upper_ref, idx_ref, w_ref, rank_ref, cnt_ref,
                  carry, *, tr):
    @pl.when(pl.program_id(0) == 0)
    def _init():
        carry[...] = jnp.zeros(carry.shape, F32)

    neg = -jnp.inf
    nt = (((1,), (1,)), ((), ()))
    h = h_ref[...]
    h_hi = h.astype(BF16)
    h_lo = (h - h_hi.astype(F32)).astype(BF16)
    both = lax.dot_general(wrt_ref[...], h_hi, nt, preferred_element_type=F32)
    logits = (both[:N_EXPERTS, :] + both[N_EXPERTS:, :]
              + lax.dot_general(wrt_ref[:N_EXPERTS, :], h_lo, nt, preferred_element_type=F32))
    scores = _sigmoid(logits)
    biased = scores + bias_ref[...]

    gs = []
    for g in range(N_GROUPS):
        blk = biased[g * GROUP_SIZE:(g + 1) * GROUP_SIZE, :]
        m1 = jnp.max(blk, axis=0, keepdims=True)
        eq = blk == m1
        n1 = jnp.sum(eq.astype(F32), axis=0, keepdims=True)
        m2 = jnp.max(jnp.where(eq, neg, blk), axis=0, keepdims=True)
        gs.append(m1 + jnp.where(n1 >= 2.0, m1, m2))
    sel = [jnp.zeros((1, tr), jnp.bool_) for _ in range(N_GROUPS)]
    for _ in range(TOPK_GROUPS):
        cur = [jnp.where(sel[g], neg, gs[g]) for g in range(N_GROUPS)]
        m = cur[0]
        for g in range(1, N_GROUPS):
            m = jnp.maximum(m, cur[g])
        found = jnp.zeros((1, tr), jnp.bool_)
        for g in range(N_GROUPS):
            pick = jnp.logical_and(cur[g] == m, jnp.logical_not(found))
            sel[g] = jnp.logical_or(sel[g], pick)
            found = jnp.logical_or(found, pick)
    cur = jnp.concatenate(
        [jnp.where(sel[g], biased[g * GROUP_SIZE:(g + 1) * GROUP_SIZE, :], neg)
         for g in range(N_GROUPS)], axis=0)

    rowid = lax.broadcasted_iota(I32, (N_EXPERTS, tr), 0)
    eligible = cur != neg
    idxs, ws = [], []
    for _ in range(TOP_K):
        m = jnp.max(cur, axis=0, keepdims=True)
        first = jnp.min(jnp.where(cur == m, rowid, N_EXPERTS), axis=0, keepdims=True)
        onehot = rowid == first
        ws.append(jnp.sum(jnp.where(onehot, scores, 0.0), axis=0, keepdims=True))
        idxs.append(first)
        cur = jnp.where(onehot, neg, cur)
    picked = jnp.logical_and(eligible, cur == neg)
    wsum = ws[0]
    for k in range(1, TOP_K):
        wsum = wsum + ws[k]

    pf = picked.astype(F32)
    before = jnp.dot(pf.astype(BF16), upper_ref[...], preferred_element_type=F32) + carry[...]
    for k in range(TOP_K):
        idx_ref[k:k + 1, :] = idxs[k]
        w_ref[k:k + 1, :] = ws[k] / wsum * ROUTED_SCALE
        rank_ref[k:k + 1, :] = jnp.sum(jnp.where(rowid == idxs[k], before, 0.0), axis=0,
                                       keepdims=True).astype(I32)
    carry[...] = carry[...] + jnp.sum(pf, axis=1, keepdims=True)
    cnt_ref[...] = carry[...]


def _route(h, wrt, bias_col, upper, tr):
    t, d = h.shape
    full = lambda shape: pl.BlockSpec(shape, lambda i: (0,) * len(shape))
    tok = lambda: pl.BlockSpec((TOP_K, tr), lambda i: (0, i))
    return pl.pallas_call(
        functools.partial(_route_kernel, tr=tr),
        grid=(t // tr,),
        in_specs=[pl.BlockSpec((tr, d), lambda i: (i, 0)),
                  full((2 * N_EXPERTS, d)), full((N_EXPERTS, 1)), full((tr, tr))],
        out_specs=[tok(), tok(), tok(), full((N_EXPERTS, 1))],
        out_shape=[jax.ShapeDtypeStruct((TOP_K, t), I32), jax.ShapeDtypeStruct((TOP_K, t), F32),
                   jax.ShapeDtypeStruct((TOP_K, t), I32),
                   jax.ShapeDtypeStruct((N_EXPERTS, 1), F32)],
        scratch_shapes=[pltpu.VMEM((N_EXPERTS, 1), F32)],
        compiler_params=_cparams(("arbitrary",), 32),
        name="route",
    )(h, wrt, bias_col, upper)


def _slots_kernel(idx_ref, rank_ref, pstart_ref, dest_ref, *, tl):
    rowid = lax.broadcasted_iota(I32, (N_EXPERTS, tl), 0)
    pstart = pstart_ref[...]
    for k in range(TOP_K):
        base = jnp.sum(jnp.where(rowid == idx_ref[k:k + 1, :], pstart, 0.0), axis=0, keepdims=True)
        dest_ref[k:k + 1, :] = base.astype(I32) + rank_ref[k:k + 1, :]


def _slots(idx, rank, pstart_col, tl):
    t = idx.shape[1]
    tok = lambda: pl.BlockSpec((TOP_K, tl), lambda i: (0, i))
    return pl.pallas_call(
        functools.partial(_slots_kernel, tl=tl),
        grid=(t // tl,),
        in_specs=[tok(), tok(), pl.BlockSpec((N_EXPERTS, 1), lambda i: (0, 0))],
        out_specs=tok(),
        out_shape=jax.ShapeDtypeStruct((TOP_K, t), I32),
        compiler_params=_cparams(("arbitrary",), 32),
        name="slots",
    )(idx, rank, pstart_col)


def _sc_scatter_rows(src, idx, n_out):
    nk, t = idx.shape
    dp = src.shape[1]
    mesh = plsc.VectorSubcoreMesh(core_axis_name="core", subcore_axis_name="subcore")

    @pl.kernel(out_type=jax.ShapeDtypeStruct((n_out, dp), src.dtype), mesh=mesh,
               name="sc_scatter")
    def _scatter(src_hbm, idx_hbm, out_hbm):
        def _body(src_vmem, idx_vmem):
            for k in range(nk):
                pltpu.sync_copy(src_vmem, out_hbm.at[idx_vmem.at[k]])

        pltpu.emit_pipeline(
            _body,
            grid=(t // SC_WINDOW,),
            in_specs=[pl.BlockSpec((SC_WINDOW, dp), lambda i: (i, 0)),
                      pl.BlockSpec((nk, SC_WINDOW), lambda i: (0, i))],
            out_specs=[],
            core_axis_name=("core", "subcore"),
            dimension_semantics=(pltpu.PARALLEL,),
        )(src_hbm, idx_hbm)

    return _scatter(src, idx)


def _expert_kernel(bmap_ref, bexp_ref, flag_ref, nrow_ref, xa_ref, xb_ref, wg_ref, wu_ref, wd_ref,
                   ya_ref, yb_ref, wgu_s, wd_s):
    i = pl.program_id(0)
    flags = flag_ref[i]
    half = D_MODEL // 2

    @pl.when(flags >= 2)
    def _cast():
        wgu_s[:, :D_EXPERT] = wg_ref[...].astype(BF16)
        wgu_s[:, D_EXPERT:] = wu_ref[...].astype(BF16)
        wd_s[...] = wd_ref[...].astype(BF16)

    nrow = nrow_ref[i]
    for c in range(EXPERT_ROWS // EXPERT_SUB):
        rows = slice(c * EXPERT_SUB, (c + 1) * EXPERT_SUB)

        @pl.when(nrow > c * EXPERT_SUB)
        def _compute():
            rid = lax.broadcasted_iota(I32, (EXPERT_SUB, xa_ref.shape[1]), 0) + c * EXPERT_SUB
            live = rid < nrow
            lo_a, hi_a = _unpack_rows(jnp.where(live, xa_ref[rows, :], jnp.uint32(0)))
            lo_b, hi_b = _unpack_rows(jnp.where(live, xb_ref[rows, :], jnp.uint32(0)))
            x = jnp.concatenate([lo_a.astype(BF16), lo_b.astype(BF16), hi_a.astype(BF16),
                                 hi_b.astype(BF16)], axis=1)
            z = jnp.dot(x, wgu_s[...], preferred_element_type=F32)
            act = _silu(z[:, :D_EXPERT]) * z[:, D_EXPERT:]
            yp = _pack_rows(jnp.dot(act.astype(BF16), wd_s[...], preferred_element_type=F32))
            ya_ref[rows, :] = yp[:, :half // 2]
            yb_ref[rows, :] = yp[:, half // 2:]

        @pl.when(nrow <= c * EXPERT_SUB)
        def _empty():
            ya_ref[rows, :] = jnp.zeros((EXPERT_SUB, ya_ref.shape[1]), ya_ref.dtype)
            yb_ref[rows, :] = jnp.zeros((EXPERT_SUB, yb_ref.shape[1]), yb_ref.dtype)


def _experts(bmap, bexp, flags, nrow, xa, xb, wg, wu, wd):
    n_slots, dq = xa.shape
    d = 4 * dq
    nb = n_slots // EXPERT_ROWS
    grid_spec = pltpu.PrefetchScalarGridSpec(
        num_scalar_prefetch=4,
        grid=(nb,),
        in_specs=[pl.BlockSpec((EXPERT_ROWS, dq), lambda i, bm, be, fl, nr: (bm[i], 0)),
                  pl.BlockSpec((EXPERT_ROWS, dq), lambda i, bm, be, fl, nr: (bm[i], 0)),
                  pl.BlockSpec((None, d, D_EXPERT), lambda i, bm, be, fl, nr: (be[i], 0, 0)),
                  pl.BlockSpec((None, d, D_EXPERT), lambda i, bm, be, fl, nr: (be[i], 0, 0)),
                  pl.BlockSpec((None, D_EXPERT, d), lambda i, bm, be, fl, nr: (be[i], 0, 0))],
        out_specs=[pl.BlockSpec((EXPERT_ROWS, dq), lambda i, bm, be, fl, nr: (i, 0))] * 2,
        scratch_shapes=[pltpu.VMEM((d, 2 * D_EXPERT), BF16), pltpu.VMEM((D_EXPERT, d), BF16)],
    )
    return pl.pallas_call(
        _expert_kernel,
        grid_spec=grid_spec,
        out_shape=[jax.ShapeDtypeStruct((n_slots, dq), U32)] * 2,
        compiler_params=_cparams(("arbitrary",), 48),
        name="experts",
    )(bmap, bexp, flags, nrow, xa, xb, wg, wu, wd)


def _sc_gather_rows(table, idx):
    n = idx.shape[0]
    dp = table.shape[1]
    mesh = plsc.VectorSubcoreMesh(core_axis_name="core", subcore_axis_name="subcore")

    @pl.kernel(out_type=jax.ShapeDtypeStruct((n, dp), table.dtype), mesh=mesh, name="sc_gather")
    def _gather(table_hbm, idx_hbm, out_hbm):
        def _body(idx_vmem, out_vmem):
            pltpu.sync_copy(table_hbm.at[idx_vmem.at[0]], out_vmem)

        pltpu.emit_pipeline(
            _body,
            grid=(n // SC_WINDOW,),
            in_specs=[pl.BlockSpec((1, SC_WINDOW), lambda i: (0, i))],
            out_specs=[pl.BlockSpec((SC_WINDOW, dp), lambda i: (i, 0))],
            core_axis_name=("core", "subcore"),
            dimension_semantics=(pltpu.PARALLEL,),
        )(idx_hbm, out_hbm)

    return _gather(table, idx.reshape(1, n))


def _combine_kernel(h_ref, w_ref, *rest):
    ya_refs = rest[:TOP_K]
    yb_refs = rest[TOP_K:2 * TOP_K]
    wsgu_ref, wsd_ref, g_ref, b_ref = rest[2 * TOP_K:2 * TOP_K + 4]
    o_ref = rest[-1]
    h = h_ref[...]
    z = jnp.dot(h.astype(BF16), wsgu_ref[...], preferred_element_type=F32)
    act = _silu(z[:, :D_SHARED]) * z[:, D_SHARED:]
    ffn = jnp.dot(act.astype(BF16), wsd_ref[...], preferred_element_type=F32)
    acc = [None] * 4
    for k in range(TOP_K):
        wk = w_ref[:, k:k + 1]
        parts = _unpack_rows(ya_refs[k][...]) + _unpack_rows(yb_refs[k][...])
        for j in range(4):
            acc[j] = parts[j] * wk if k == 0 else acc[j] + parts[j] * wk
    ffn = ffn + jnp.concatenate([acc[0], acc[2], acc[1], acc[3]], axis=1)
    o_ref[...] = _layer_norm(DN_ALPHA * h + ffn, g_ref[...], b_ref[...])


def _combine(h, w_t, yga, ygb, wsgu, wsd, g, b, tm, chunk, n_chunks, prev):
    t, d = h.shape
    nt = t // n_chunks // tm
    base = chunk * nt
    full = lambda shape: pl.BlockSpec(shape, lambda i: (0,) * len(shape))
    y_spec = lambda k: pl.BlockSpec((tm, d // 4), lambda i: (k * nt + i, 0))
    in_specs = ([pl.BlockSpec((tm, d), lambda i: (base + i, 0)),
                 pl.BlockSpec((tm, TOP_K), lambda i: (base + i, 0))]
                + [y_spec(k) for k in range(TOP_K)] * 2
                + [full((d, 2 * D_SHARED)), full((D_SHARED, d)), full((1, d)), full((1, d))])
    args = [h, w_t, *([yga] * TOP_K), *([ygb] * TOP_K), wsgu, wsd, g, b]
    aliases = {}
    if prev is not None:
        in_specs.append(pl.BlockSpec(memory_space=pl.ANY))
        aliases = {len(args): 0}
        args.append(prev)
    return pl.pallas_call(
        _combine_kernel,
        grid=(nt,),
        in_specs=in_specs,
        out_specs=pl.BlockSpec((tm, d), lambda i: (base + i, 0)),
        out_shape=jax.ShapeDtypeStruct((t, d), F32),
        input_output_aliases=aliases,
        compiler_params=_cparams(("arbitrary",), 40),
        name="combine",
    )(*args)


def _block_diag4(w):
    n = w.shape[0] // 4
    w4 = w.reshape(n, 4, LRU_BLOCK_DIM, LRU_BLOCK_DIM)
    eye = jnp.eye(4, dtype=w.dtype)
    return jnp.einsum('gaij,ab->gaibj', w4, eye).reshape(n, 4 * LRU_BLOCK_DIM, 4 * LRU_BLOCK_DIM)


def _layer(x, w_in, conv_w, conv_b, lru_w_a, lru_b_a, lru_w_x, lru_b_x, lru_lambda, w_lru_out,
           gla_w_gate_up, gla_b_gate, gla_norm_g, w_gla_out, w_mix_out, ln1_g, ln1_b,
           w_router, router_bias, w_exp_gate, w_exp_up, w_exp_down, w_sh_gate, w_sh_up, w_sh_down,
           ln2_g, ln2_b):
    batch, seq, d = x.shape
    t = batch * seq
    x2 = x.reshape(t, d)
    row = lambda v: v.reshape(1, -1)

    glow_lo = 2 * D_LRU + 2 * GLA_QK_DIM + 2 * GLA_V_DIM
    glow_hi = glow_lo + GLA_GATE_RANK
    w_p = jnp.concatenate(
        [w_in[:, :glow_lo], w_in[:, glow_hi:], w_in[:, glow_lo:glow_hi],
         jnp.zeros((d, GLOW_PAD - GLA_GATE_RANK), w_in.dtype)], axis=1).astype(BF16)
    p = _proj(x2, w_p, tm=min(512, t))

    wg = jnp.concatenate([_block_diag4(lru_w_a), _block_diag4(lru_w_x)], axis=2).astype(BF16)
    ts_lru = min(512, seq)
    lru_o = _lru(p, conv_w, row(conv_b), wg, row(lru_b_a), row(lru_b_x), row(lru_lambda),
                 batch, seq, ts_lru)

    wup = jnp.concatenate(
        [gla_w_gate_up, jnp.zeros((GLOW_PAD - GLA_GATE_RANK, GLA_QK_DIM), gla_w_gate_up.dtype)],
        axis=0).astype(BF16)
    tri = jnp.tril(jnp.ones((GLA_CHUNK, GLA_CHUNK), BF16))
    ts_gla = min(512, seq)
    gla_o = _gla(p, wup, row(gla_b_gate), row(gla_norm_g), tri, batch, seq, ts_gla)

    h, hpa, hpb = _merge(lru_o, gla_o, p, x2, w_lru_out.astype(BF16), w_gla_out.astype(BF16),
                         w_mix_out.astype(BF16), row(ln1_g), row(ln1_b), tm=min(512, t))

    tr = min(256, t)
    upper = jnp.triu(jnp.ones((tr, tr), F32), k=1).astype(BF16)
    wrt = w_router.T
    wrt_top = lax.bitcast_convert_type(
        lax.bitcast_convert_type(wrt, U32) & jnp.uint32(0xFFFF0000), F32)
    wrt_hi = wrt_top.astype(BF16)
    wrt_lo = (wrt - wrt_top).astype(BF16)
    idx, w, rank, cnt = _route(h, jnp.concatenate([wrt_hi, wrt_lo], axis=0),
                               router_bias.reshape(-1, 1), upper, tr)

    counts = cnt[:, 0].astype(I32)
    padded = (counts + EXPERT_ROWS - 1) // EXPERT_ROWS * EXPERT_ROWS
    pend = jnp.cumsum(padded)
    pstart = pend - padded
    dest = _slots(idx, rank, pstart.astype(F32).reshape(-1, 1), tl=min(512, t))
    n_blocks = (t * TOP_K) // EXPERT_ROWS + N_EXPERTS
    n_slots = n_blocks * EXPERT_ROWS
    blk = jnp.arange(n_blocks, dtype=I32)
    n_used = pend[-1] // EXPERT_ROWS
    bmap = jnp.minimum(blk, n_used - 1)
    bexp = jnp.minimum(
        jnp.sum((pend[None, :] <= (bmap * EXPERT_ROWS)[:, None]).astype(I32), axis=1),
        N_EXPERTS - 1)
    first = jnp.concatenate([jnp.ones((1,), I32), (bexp[1:] != bexp[:-1]).astype(I32)])
    flags = (blk < n_used).astype(I32) + 2 * first

    live_end = jnp.sum(jnp.where(bexp[:, None] == jnp.arange(N_EXPERTS, dtype=I32)[None, :],
                                 (pstart + counts)[None, :], 0), axis=1)
    nrow = jnp.where(blk < n_used,
                     jnp.clip(live_end - bmap * EXPERT_ROWS, 0, EXPERT_ROWS), 0).astype(I32)

    xa = _sc_scatter_rows(hpa, dest, n_slots)
    xb = _sc_scatter_rows(hpb, dest, n_slots)
    ya, yb = _experts(bmap, bexp, flags, nrow, xa, xb, w_exp_gate, w_exp_up, w_exp_down)

    wsgu = jnp.concatenate([w_sh_gate, w_sh_up], axis=1).astype(BF16)
    wsd = w_sh_down.astype(BF16)
    w_t = w.T
    tm_c = min(512, t)
    n_chunks = COMBINE_CHUNKS if t % (COMBINE_CHUNKS * tm_c) == 0 else 1
    tc = t // n_chunks
    out = None
    for c in range(n_chunks):
        idx_c = dest[:, c * tc:(c + 1) * tc].reshape(-1)
        out = _combine(h, w_t, _sc_gather_rows(ya, idx_c), _sc_gather_rows(yb, idx_c), wsgu, wsd,
                       row(ln2_g), row(ln2_b), tm_c, c, n_chunks, out)
    return out.reshape(batch, seq, d)


def kernel(x, w_in, conv_w, conv_b, lru_w_a, lru_b_a, lru_w_x, lru_b_x, lru_lambda, w_lru_out,
           gla_w_gate_up, gla_b_gate, gla_norm_g, w_gla_out, w_mix_out, ln1_g, ln1_b,
           w_router, router_bias, w_exp_gate, w_exp_up, w_exp_down, w_sh_gate, w_sh_up, w_sh_down,
           ln2_g, ln2_b):
    params = (w_in, conv_w, conv_b, lru_w_a, lru_b_a, lru_w_x, lru_b_x, lru_lambda, w_lru_out,
              gla_w_gate_up, gla_b_gate, gla_norm_g, w_gla_out, w_mix_out, ln1_g, ln1_b,
              w_router, router_bias, w_exp_gate, w_exp_up, w_exp_down, w_sh_gate, w_sh_up,
              w_sh_down, ln2_g, ln2_b)
    h = x
    for l in range(DEPTH):
        h = _layer(h, *(p[l] for p in params))
    return h
```

```python
import functools

import jax
import jax.numpy as jnp
from jax import lax
from jax.experimental import pallas as pl
from jax.experimental.pallas import tpu as pltpu
from jax.experimental.pallas import tpu_sc as plsc

F32 = jnp.float32
BF16 = jnp.bfloat16
I32 = jnp.int32
U32 = jnp.uint32

D_MODEL = 1024
D_LRU = D_MODEL
LRU_BLOCKS = 16
LRU_BLOCK_DIM = D_LRU // LRU_BLOCKS
CONV_WIDTH = 4
LRU_C = 8.0
GLA_HEADS = 4
GLA_QK_DIM = D_MODEL // 2
GLA_V_DIM = D_MODEL
GLA_DK = GLA_QK_DIM // GLA_HEADS
GLA_DV = GLA_V_DIM // GLA_HEADS
GLA_GATE_RANK = 16
GLA_TAU = 16.0
GLA_CHUNK = 64
N_EXPERTS = 256
TOP_K = 8
N_GROUPS = 8
GROUP_SIZE = N_EXPERTS // N_GROUPS
TOPK_GROUPS = 4
D_EXPERT = 256
D_SHARED = 256
ROUTED_SCALE = 2.5
DEPTH = 1
DN_ALPHA = (2.0 * DEPTH) ** 0.25
LN_EPS = 1e-5
RMS_EPS = 1e-6

LANES = 128
SUBLANES = 8
GLOW_PAD = LANES
N_MAIN = 2 * D_LRU + 2 * GLA_QK_DIM + 2 * GLA_V_DIM + 2 * D_MODEL
N_PROJ = N_MAIN + GLOW_PAD
PROJ_CHUNK = 512

EXPERT_ROWS = 1024
EXPERT_SUB = 512
SC_WINDOW = 128
COMBINE_CHUNKS = 4


def _cparams(semantics, vmem_mib):
    return pltpu.CompilerParams(dimension_semantics=semantics,
                                vmem_limit_bytes=vmem_mib * 1024 * 1024)


def _sigmoid(x):
    return jax.nn.sigmoid(x)


def _silu(x):
    return x * jax.nn.sigmoid(x)


def _gelu_tanh(x):
    c = 0.7978845608028654
    return x * (0.5 * (1.0 + jnp.tanh(c * (x + 0.044715 * (x * x * x)))))


def _pack_rows(v):
    n = v.shape[1] // 2
    lo = lax.bitcast_convert_type(v[:, :n].astype(BF16).astype(F32), U32)
    hi = lax.bitcast_convert_type(v[:, n:].astype(BF16).astype(F32), U32)
    return (lo >> 16) | (hi & jnp.uint32(0xFFFF0000))


def _unpack_rows(p):
    lo = lax.bitcast_convert_type(p << 16, F32)
    hi = lax.bitcast_convert_type(p & jnp.uint32(0xFFFF0000), F32)
    return lo, hi


def _layer_norm(z, g, b):
    mu = jnp.mean(z, axis=-1, keepdims=True)
    zc = z - mu
    var = jnp.mean(zc * zc, axis=-1, keepdims=True)
    return zc * lax.rsqrt(var + LN_EPS) * g + b


def _proj_kernel(x_ref, w_ref, o_ref):
    xb = x_ref[...].astype(BF16)
    for j in range(0, N_MAIN, PROJ_CHUNK):
        o_ref[:, j:j + PROJ_CHUNK] = jnp.dot(
            xb, w_ref[:, j:j + PROJ_CHUNK], preferred_element_type=F32).astype(BF16)
    o_ref[:, N_MAIN:] = jnp.dot(xb, w_ref[:, N_MAIN:], preferred_element_type=F32).astype(BF16)


def _proj(x2, w_p, tm):
    t, d = x2.shape
    return pl.pallas_call(
        _proj_kernel,
        grid=(t // tm,),
        in_specs=[pl.BlockSpec((tm, d), lambda i: (i, 0)),
                  pl.BlockSpec((d, N_PROJ), lambda i: (0, 0), pipeline_mode=pl.Buffered(1))],
        out_specs=pl.BlockSpec((tm, N_PROJ), lambda i: (i, 0)),
        out_shape=jax.ShapeDtypeStruct((t, N_PROJ), BF16),
        compiler_params=_cparams(("arbitrary",), 48),
        name="proj",
    )(x2, w_p)


def _lru_kernel(xa_ref, ga_ref, cw_ref, cb_ref, wg_ref, ba_ref, bx_ref, lam_ref, o_ref,
                xbuf, xc_s, a_s, u_s, cin_s, hnat, carry, *, ts):
    nb = ts // SUBLANES
    nslab = D_LRU // LANES
    grp = 4 * LRU_BLOCK_DIM
    hist = SUBLANES

    @pl.when(pl.program_id(1) == 0)
    def _init():
        for s in range(nslab):
            xbuf[s, 0:hist, :] = jnp.zeros((hist, LANES), F32)
        carry[...] = jnp.zeros((1, D_LRU), F32)

    for s in range(nslab):
        cs_ = slice(s * LANES, (s + 1) * LANES)
        xbuf[s, hist:, :] = xa_ref[:, cs_].astype(F32)
        taps = {}
        for r in range(SUBLANES):
            acc = cb_ref[:, cs_]
            for j in range(CONV_WIDTH):
                off = hist - (CONV_WIDTH - 1) + r + j
                if off not in taps:
                    taps[off] = xbuf[s, pl.ds(off, nb, stride=SUBLANES), :]
                acc = acc + cw_ref[j:j + 1, cs_] * taps[off]
            xc_s[r * nb:(r + 1) * nb, cs_] = acc
        xbuf[s, 0:hist, :] = xbuf[s, ts:ts + hist, :]

    lam = lam_ref[...]
    sp = jnp.maximum(-lam, 0.0) + jnp.log1p(jnp.exp(-jnp.abs(lam)))
    for g in range(D_LRU // grp):
        sl = slice(g * grp, (g + 1) * grp)
        xg = xc_s[:, sl]
        z = jnp.dot(xg.astype(BF16), wg_ref[g], preferred_element_type=F32)
        r = _sigmoid(z[:, :grp] + ba_ref[:, sl])
        i = _sigmoid(z[:, grp:] + bx_ref[:, sl])
        log_a = (-LRU_C * r) * sp[:, sl]
        a = jnp.exp(log_a)
        a_s[:, sl] = a
        u_s[:, sl] = jnp.sqrt(-jnp.tanh(log_a) * (a * a + 1.0)) * (i * xg)

    for r in range(1, SUBLANES):
        prev = slice((r - 1) * nb, r * nb)
        cur = slice(r * nb, (r + 1) * nb)
        a_r = a_s[cur, :]
        u_s[cur, :] = a_r * u_s[prev, :] + u_s[cur, :]
        a_s[cur, :] = a_r * a_s[prev, :]

    last = (SUBLANES - 1) * nb

    def _carry_step(b, c):
        cin_s[pl.ds(b, 1), :] = c
        return a_s[pl.ds(last + b, 1), :] * c + u_s[pl.ds(last + b, 1), :]

    carry[...] = lax.fori_loop(0, nb, _carry_step, carry[...])

    cin = cin_s[...]
    for r in range(SUBLANES):
        cur = slice(r * nb, (r + 1) * nb)
        h_r = u_s[cur, :] + a_s[cur, :] * cin
        for s in range(nslab):
            hnat[s, pl.ds(r, nb, stride=SUBLANES), :] = h_r[:, s * LANES:(s + 1) * LANES]

    for s in range(nslab):
        cs_ = slice(s * LANES, (s + 1) * LANES)
        o_ref[:, cs_] = (hnat[s] * _gelu_tanh(ga_ref[:, cs_].astype(F32))).astype(BF16)


def _lru(p, cw, cb, wg, ba, bx, lam, batch, seq, ts):
    t = batch * seq
    nst = seq // ts
    nslab = D_LRU // LANES
    row = lambda b, s: b * nst + s
    full = lambda shape: pl.BlockSpec(shape, lambda b, s: (0,) * len(shape))
    return pl.pallas_call(
        functools.partial(_lru_kernel, ts=ts),
        grid=(batch, nst),
        in_specs=[pl.BlockSpec((ts, D_LRU), lambda b, s: (row(b, s), 0)),
                  pl.BlockSpec((ts, D_LRU), lambda b, s: (row(b, s), 1)),
                  full((CONV_WIDTH, D_LRU)), full((1, D_LRU)),
                  full(wg.shape), full((1, D_LRU)), full((1, D_LRU)), full((1, D_LRU))],
        out_specs=pl.BlockSpec((ts, D_LRU), lambda b, s: (row(b, s), 0)),
        out_shape=jax.ShapeDtypeStruct((t, D_LRU), BF16),
        scratch_shapes=[pltpu.VMEM((nslab, ts + SUBLANES, LANES), F32),
                        pltpu.VMEM((ts, D_LRU), F32),
                        pltpu.VMEM((ts, D_LRU), F32),
                        pltpu.VMEM((ts, D_LRU), F32),
                        pltpu.VMEM((ts // SUBLANES, D_LRU), F32),
                        pltpu.VMEM((nslab, ts, LANES), F32),
                        pltpu.VMEM((1, D_LRU), F32)],
        compiler_params=_cparams(("arbitrary", "arbitrary"), 40),
        name="lru",
    )(p, p, cw, cb, wg, ba, bx, lam)


def _gla_kernel(q_ref, k_ref, v_ref, r_ref, gl_ref, wup_ref, bg_ref, ng_ref, tri_ref, o_ref,
                st_ref, *, ts):
    @pl.when(pl.program_id(1) == 0)
    def _init():
        st_ref[...] = jnp.zeros(st_ref.shape, F32)

    logit = jnp.dot(gl_ref[...], wup_ref[...], preferred_element_type=F32) + bg_ref[...]
    log_a = (jnp.minimum(logit, 0.0) - jnp.log1p(jnp.exp(-jnp.abs(logit)))) * (1.0 / GLA_TAU)
    tri = tri_ref[...]
    ri = lax.broadcasted_iota(I32, (GLA_CHUNK, GLA_CHUNK), 0)
    ci = lax.broadcasted_iota(I32, (GLA_CHUNK, GLA_CHUNK), 1)
    causal = ri >= ci
    nt = (((1,), (1,)), ((), ()))
    tn = (((0,), (0,)), ((), ()))
    for c in range(ts // GLA_CHUNK):
        rs = slice(c * GLA_CHUNK, (c + 1) * GLA_CHUNK)
        la = log_a[rs, :]
        la_hi = la.astype(BF16)
        la_r = la - la_hi.astype(F32)
        la_mid = la_r.astype(BF16)
        la_lo = (la_r - la_mid.astype(F32)).astype(BF16)
        parts = jnp.dot(tri, jnp.concatenate([la_hi, la_mid, la_lo], axis=1),
                        preferred_element_type=F32)
        bcum = (parts[:, :GLA_QK_DIM] + parts[:, GLA_QK_DIM:2 * GLA_QK_DIM]
                + parts[:, 2 * GLA_QK_DIM:])
        bl = bcum[GLA_CHUNK - 1:GLA_CHUNK, :]
        kf = k_ref[rs, :].astype(F32)
        qd = (q_ref[rs, :].astype(F32) * (GLA_DK ** -0.5)) * jnp.exp(bcum)
        ki = kf * jnp.exp(-bcum)
        ke = kf * jnp.exp(bl - bcum)
        gl = jnp.exp(bl)
        for h in range(GLA_HEADS):
            hs = slice(h * GLA_DK, (h + 1) * GLA_DK)
            vs = slice(h * GLA_DV, (h + 1) * GLA_DV)
            qd_h = qd[:, hs].astype(BF16)
            sc = lax.dot_general(qd_h, ki[:, hs].astype(BF16), nt, preferred_element_type=F32)
            sc = jnp.where(causal, sc, 0.0)
            v_h = v_ref[rs, vs]
            st_t = st_ref[h]
            o = jnp.dot(sc.astype(BF16), v_h, preferred_element_type=F32)
            o = o + lax.dot_general(qd_h, st_t.astype(BF16), nt, preferred_element_type=F32)
            st_ref[h] = st_t * gl[:, hs] + lax.dot_general(
                v_h, ke[:, hs].astype(BF16), tn, preferred_element_type=F32)
            ms = jnp.mean(o * o, axis=-1, keepdims=True)
            o = o * lax.rsqrt(ms + RMS_EPS) * ng_ref[:, vs]
            o_ref[rs, vs] = (o * _silu(r_ref[rs, vs].astype(F32))).astype(BF16)


def _gla(p, wup, bg, ng, tri, batch, seq, ts):
    t = batch * seq
    nst = seq // ts
    row = lambda b, s: b * nst + s
    full = lambda shape: pl.BlockSpec(shape, lambda b, s: (0,) * len(shape))
    q_blk = (2 * D_LRU) // GLA_QK_DIM
    v_blk = (2 * D_LRU + 2 * GLA_QK_DIM) // GLA_V_DIM
    gl_blk = N_MAIN // GLOW_PAD
    return pl.pallas_call(
        functools.partial(_gla_kernel, ts=ts),
        grid=(batch, nst),
        in_specs=[pl.BlockSpec((ts, GLA_QK_DIM), lambda b, s: (row(b, s), q_blk)),
                  pl.BlockSpec((ts, GLA_QK_DIM), lambda b, s: (row(b, s), q_blk + 1)),
                  pl.BlockSpec((ts, GLA_V_DIM), lambda b, s: (row(b, s), v_blk)),
                  pl.BlockSpec((ts, GLA_V_DIM), lambda b, s: (row(b, s), v_blk + 1)),
                  pl.BlockSpec((ts, GLOW_PAD), lambda b, s: (row(b, s), gl_blk)),
                  full((GLOW_PAD, GLA_QK_DIM)), full((1, GLA_QK_DIM)), full((1, GLA_V_DIM)),
                  full((GLA_CHUNK, GLA_CHUNK))],
        out_specs=pl.BlockSpec((ts, GLA_V_DIM), lambda b, s: (row(b, s), 0)),
        out_shape=jax.ShapeDtypeStruct((t, GLA_V_DIM), BF16),
        scratch_shapes=[pltpu.VMEM((GLA_HEADS, GLA_DV, GLA_DK), F32)],
        compiler_params=_cparams(("arbitrary", "arbitrary"), 40),
        name="gla",
    )(p, p, p, p, p, wup, bg, ng, tri)


def _merge_kernel(lru_ref, gla_ref, ga_ref, gb_ref, x_ref, wa_ref, wb_ref, wo_ref, g_ref, b_ref,
                  h_ref, hpa_ref, hpb_ref):
    ya = jnp.dot(lru_ref[...], wa_ref[...], preferred_element_type=F32)
    yb = jnp.dot(gla_ref[...], wb_ref[...], preferred_element_type=F32)
    merged = _sigmoid(ga_ref[...].astype(F32)) * ya + _sigmoid(gb_ref[...].astype(F32)) * yb
    mix = jnp.dot(merged.astype(BF16), wo_ref[...], preferred_element_type=F32)
    h = _layer_norm(DN_ALPHA * x_ref[...] + mix, g_ref[...], b_ref[...])
    h_ref[...] = h
    hp = _pack_rows(h)
    quarter = hp.shape[1] // 2
    hpa_ref[...] = hp[:, :quarter]
    hpb_ref[...] = hp[:, quarter:]


def _merge(lru_o, gla_o, p, x2, wa, wb, wo, g, b, tm):
    t, d = x2.shape
    ga_blk = (2 * D_LRU + 2 * GLA_QK_DIM + 2 * GLA_V_DIM) // D_MODEL
    rowb = lambda c: pl.BlockSpec((tm, d), lambda i: (i, c))
    full = lambda shape: pl.BlockSpec(shape, lambda i: (0,) * len(shape),
                                      pipeline_mode=pl.Buffered(1))
    return pl.pallas_call(
        _merge_kernel,
        grid=(t // tm,),
        in_specs=[rowb(0), rowb(0), rowb(ga_blk), rowb(ga_blk + 1), rowb(0),
                  full((d, d)), full((d, d)), full((d, d)), full((1, d)), full((1, d))],
        out_specs=[rowb(0)] + [pl.BlockSpec((tm, d // 4), lambda i: (i, 0))] * 2,
        out_shape=[jax.ShapeDtypeStruct((t, d), F32)]
                  + [jax.ShapeDtypeStruct((t, d // 4), U32)] * 2,
        compiler_params=_cparams(("arbitrary",), 48),
        name="merge",
    )(lru_o, gla_o, p, p, x2, wa, wb, wo, g, b)


def _route_kernel(h_ref, wrt_ref, bias_ref, upper_ref, idx_ref, w_ref, rank_ref, cnt_ref,
                  carry, *, tr):
    @pl.when(pl.program_id(0) == 0)
    def _init():
        carry[...] = jnp.zeros(carry.shape, F32)

    neg = -jnp.inf
    nt = (((1,), (1,)), ((), ()))
    h = h_ref[...]
    h_hi = h.astype(BF16)
    h_lo = (h - h_hi.astype(F32)).astype(BF16)
    both = lax.dot_general(wrt_ref[...], h_hi, nt, preferred_element_type=F32)
    logits = (both[:N_EXPERTS, :] + both[N_EXPERTS:, :]
              + lax.dot_general(wrt_ref[:N_EXPERTS, :], h_lo, nt, preferred_element_type=F32))
    scores = _sigmoid(logits)
    biased = scores + bias_ref[...]

    gs = []
    for g in range(N_GROUPS):
        blk = biased[g * GROUP_SIZE:(g + 1) * GROUP_SIZE, :]
        m1 = jnp.max(blk, axis=0, keepdims=True)
        eq = blk == m1
        n1 = jnp.sum(eq.astype(F32), axis=0, keepdims=True)
        m2 = jnp.max(jnp.where(eq, neg, blk), axis=0, keepdims=True)
        gs.append(m1 + jnp.where(n1 >= 2.0, m1, m2))
    sel = [jnp.zeros((1, tr), jnp.bool_) for _ in range(N_GROUPS)]
    for _ in range(TOPK_GROUPS):
        cur = [jnp.where(sel[g], neg, gs[g]) for g in range(N_GROUPS)]
        m = cur[0]
        for g in range(1, N_GROUPS):
            m = jnp.maximum(m, cur[g])
        found = jnp.zeros((1, tr), jnp.bool_)
        for g in range(N_GROUPS):
            pick = jnp.logical_and(cur[g] == m, jnp.logical_not(found))
            sel[g] = jnp.logical_or(sel[g], pick)
            found = jnp.logical_or(found, pick)
    cur = jnp.concatenate(
        [jnp.where(sel[g], biased[g * GROUP_SIZE:(g + 1) * GROUP_SIZE, :], neg)
         for g in range(N_GROUPS)], axis=0)

    rowid = lax.broadcasted_iota(I32, (N_EXPERTS, tr), 0)
    eligible = cur != neg
    idxs, ws = [], []
    for _ in range(TOP_K):
        m = jnp.max(cur, axis=0, keepdims=True)
        first = jnp.min(jnp.where(cur == m, rowid, N_EXPERTS), axis=0, keepdims=True)
        onehot = rowid == first
        ws.append(jnp.sum(jnp.where(onehot, scores, 0.0), axis=0, keepdims=True))
        idxs.append(first)
        cur = jnp.where(onehot, neg, cur)
    picked = jnp.logical_and(eligible, cur == neg)
    wsum = ws[0]
    for k in range(1, TOP_K):
        wsum = wsum + ws[k]

    pf = picked.astype(F32)
    before = jnp.dot(pf.astype(BF16), upper_ref[...], preferred_element_type=F32) + carry[...]
    for k in range(TOP_K):
        idx_ref[k:k + 1, :] = idxs[k]
        w_ref[k:k + 1, :] = ws[k] / wsum * ROUTED_SCALE
        rank_ref[k:k + 1, :] = jnp.sum(jnp.where(rowid == idxs[k], before, 0.0), axis=0,
                                       keepdims=True).astype(I32)
    carry[...] = carry[...] + jnp.sum(pf, axis=1, keepdims=True)
    cnt_ref[...] = carry[...]


def _route(h, wrt, bias_col, upper, tr):
    t, d = h.shape
    full = lambda shape: pl.BlockSpec(shape, lambda i: (0,) * len(shape))
    tok = lambda: pl.BlockSpec((TOP_K, tr), lambda i: (0, i))
    return pl.pallas_call(
        functools.partial(_route_kernel, tr=tr),
        grid=(t // tr,),
        in_specs=[pl.BlockSpec((tr, d), lambda i: (i, 0)),
                  full((2 * N_EXPERTS, d)), full((N_EXPERTS, 1)), full((tr, tr))],
        out_specs=[tok(), tok(), tok(), full((N_EXPERTS, 1))],
        out_shape=[jax.ShapeDtypeStruct((TOP_K, t), I32), jax.ShapeDtypeStruct((TOP_K, t), F32),
                   jax.ShapeDtypeStruct((TOP_K, t), I32),
                   jax.ShapeDtypeStruct((N_EXPERTS, 1), F32)],
        scratch_shapes=[pltpu.VMEM((N_EXPERTS, 1), F32)],
        compiler_params=_cparams(("arbitrary",), 32),
        name="route",
    )(h, wrt, bias_col, upper)


def _slots_kernel(idx_ref, rank_ref, pstart_ref, dest_ref, *, tl):
    rowid = lax.broadcasted_iota(I32, (N_EXPERTS, tl), 0)
    pstart = pstart_ref[...]
    for k in range(TOP_K):
        base = jnp.sum(jnp.where(rowid == idx_ref[k:k + 1, :], pstart, 0.0), axis=0, keepdims=True)
        dest_ref[k:k + 1, :] = base.astype(I32) + rank_ref[k:k + 1, :]


def _slots(idx, rank, pstart_col, tl):
    t = idx.shape[1]
    tok = lambda: pl.BlockSpec((TOP_K, tl), lambda i: (0, i))
    return pl.pallas_call(
        functools.partial(_slots_kernel, tl=tl),
        grid=(t // tl,),
        in_specs=[tok(), tok(), pl.BlockSpec((N_EXPERTS, 1), lambda i: (0, 0))],
        out_specs=tok(),
        out_shape=jax.ShapeDtypeStruct((TOP_K, t), I32),
        compiler_params=_cparams(("arbitrary",), 32),
        name="slots",
    )(idx, rank, pstart_col)


def _sc_scatter_rows(src, idx, n_out):
    nk, t = idx.shape
    dp = src.shape[1]
    mesh = plsc.VectorSubcoreMesh(core_axis_name="core", subcore_axis_name="subcore")

    @pl.kernel(out_type=jax.ShapeDtypeStruct((n_out, dp), src.dtype), mesh=mesh,
               name="sc_scatter")
    def _scatter(src_hbm, idx_hbm, out_hbm):
        def _body(src_vmem, idx_vmem):
            for k in range(nk):
                pltpu.sync_copy(src_vmem, out_hbm.at[idx_vmem.at[k]])

        pltpu.emit_pipeline(
            _body,
            grid=(t // SC_WINDOW,),
            in_specs=[pl.BlockSpec((SC_WINDOW, dp), lambda i: (i, 0)),
                      pl.BlockSpec((nk, SC_WINDOW), lambda i: (0, i))],
            out_specs=[],
            core_axis_name=("core", "subcore"),
            dimension_semantics=(pltpu.PARALLEL,),
        )(src_hbm, idx_hbm)

    return _scatter(src, idx)


def _expert_kernel(bmap_ref, bexp_ref, flag_ref, nrow_ref, xa_ref, xb_ref, wg_ref, wu_ref, wd_ref,
                   ya_ref, yb_ref, wgu_s, wd_s):
    i = pl.program_id(0)
    flags = flag_ref[i]
    half = D_MODEL // 2

    @pl.when(flags >= 2)
    def _cast():
        wgu_s[:, :D_EXPERT] = wg_ref[...].astype(BF16)
        wgu_s[:, D_EXPERT:] = wu_ref[...].astype(BF16)
        wd_s[...] = wd_ref[...].astype(BF16)

    nrow = nrow_ref[i]
    for c in range(EXPERT_ROWS // EXPERT_SUB):
        rows = slice(c * EXPERT_SUB, (c + 1) * EXPERT_SUB)

        @pl.when(nrow > c * EXPERT_SUB)
        def _compute():
            rid = lax.broadcasted_iota(I32, (EXPERT_SUB, xa_ref.shape[1]), 0) + c * EXPERT_SUB
            live = rid < nrow
            lo_a, hi_a = _unpack_rows(jnp.where(live, xa_ref[rows, :], jnp.uint32(0)))
            lo_b, hi_b = _unpack_rows(jnp.where(live, xb_ref[rows, :], jnp.uint32(0)))
            x = jnp.concatenate([lo_a.astype(BF16), lo_b.astype(BF16), hi_a.astype(BF16),
                                 hi_b.astype(BF16)], axis=1)
            z = jnp.dot(x, wgu_s[...], preferred_element_type=F32)
            act = _silu(z[:, :D_EXPERT]) * z[:, D_EXPERT:]
            yp = _pack_rows(jnp.dot(act.astype(BF16), wd_s[...], preferred_element_type=F32))
            ya_ref[rows, :] = yp[:, :half // 2]
            yb_ref[rows, :] = yp[:, half // 2:]

        @pl.when(nrow <= c * EXPERT_SUB)
        def _empty():
            ya_ref[rows, :] = jnp.zeros((EXPERT_SUB, ya_ref.shape[1]), ya_ref.dtype)
            yb_ref[rows, :] = jnp.zeros((EXPERT_SUB, yb_ref.shape[1]), yb_ref.dtype)


def _experts(n_used, bmap, bexp, flags, nrow, xa, xb, wg, wu, wd):
    n_slots, dq = xa.shape
    d = 4 * dq
    grid_spec = pltpu.PrefetchScalarGridSpec(
        num_scalar_prefetch=4,
        grid=(n_used,),
        in_specs=[pl.BlockSpec((EXPERT_ROWS, dq), lambda i, bm, be, fl, nr: (bm[i], 0)),
                  pl.BlockSpec((EXPERT_ROWS, dq), lambda i, bm, be, fl, nr: (bm[i], 0)),
                  pl.BlockSpec((None, d, D_EXPERT), lambda i, bm, be, fl, nr: (be[i], 0, 0)),
                  pl.BlockSpec((None, d, D_EXPERT), lambda i, bm, be, fl, nr: (be[i], 0, 0)),
                  pl.BlockSpec((None, D_EXPERT, d), lambda i, bm, be, fl, nr: (be[i], 0, 0))],
        out_specs=[pl.BlockSpec((EXPERT_ROWS, dq), lambda i, bm, be, fl, nr: (i, 0))] * 2,
        scratch_shapes=[pltpu.VMEM((d, 2 * D_EXPERT), BF16), pltpu.VMEM((D_EXPERT, d), BF16)],
    )
    return pl.pallas_call(
        _expert_kernel,
        grid_spec=grid_spec,
        out_shape=[jax.ShapeDtypeStruct((n_slots, dq), U32)] * 2,
        compiler_params=_cparams(("arbitrary",), 48),
        name="experts",
    )(bmap, bexp, flags, nrow, xa, xb, wg, wu, wd)


def _sc_gather_rows(table, idx):
    n = idx.shape[0]
    dp = table.shape[1]
    mesh = plsc.VectorSubcoreMesh(core_axis_name="core", subcore_axis_name="subcore")

    @pl.kernel(out_type=jax.ShapeDtypeStruct((n, dp), table.dtype), mesh=mesh, name="sc_gather")
    def _gather(table_hbm, idx_hbm, out_hbm):
        def _body(idx_vmem, out_vmem):
            pltpu.sync_copy(table_hbm.at[idx_vmem.at[0]], out_vmem)

        pltpu.emit_pipeline(
            _body,
            grid=(n // SC_WINDOW,),
            in_specs=[pl.BlockSpec((1, SC_WINDOW), lambda i: (0, i))],
            out_specs=[pl.BlockSpec((SC_WINDOW, dp), lambda i: (i, 0))],
            core_axis_name=("core", "subcore"),
            dimension_semantics=(pltpu.PARALLEL,),
        )(idx_hbm, out_hbm)

    return _gather(table, idx.reshape(1, n))


def _combine_kernel(h_ref, w_ref, *rest):
    ya_refs = rest[:TOP_K]
    yb_refs = rest[TOP_K:2 * TOP_K]
    wsgu_ref, wsd_ref, g_ref, b_ref = rest[2 * TOP_K:2 * TOP_K + 4]
    o_ref = rest[-1]
    h = h_ref[...]
    z = jnp.dot(h.astype(BF16), wsgu_ref[...], preferred_element_type=F32)
    act = _silu(z[:, :D_SHARED]) * z[:, D_SHARED:]
    ffn = jnp.dot(act.astype(BF16), wsd_ref[...], preferred_element_type=F32)
    acc = [None] * 4
    for k in range(TOP_K):
        wk = w_ref[:, k:k + 1]
        parts = _unpack_rows(ya_refs[k][...]) + _unpack_rows(yb_refs[k][...])
        for j in range(4):
            acc[j] = parts[j] * wk if k == 0 else acc[j] + parts[j] * wk
    ffn = ffn + jnp.concatenate([acc[0], acc[2], acc[1], acc[3]], axis=1)
    o_ref[...] = _layer_norm(DN_ALPHA * h + ffn, g_ref[...], b_ref[...])


def _combine(h, w_t, yga, ygb, wsgu, wsd, g, b, tm, chunk, n_chunks, prev):
    t, d = h.shape
    nt = t // n_chunks // tm
    base = chunk * nt
    full = lambda shape: pl.BlockSpec(shape, lambda i: (0,) * len(shape))
    y_spec = lambda k: pl.BlockSpec((tm, d // 4), lambda i: (k * nt + i, 0))
    in_specs = ([pl.BlockSpec((tm, d), lambda i: (base + i, 0)),
                 pl.BlockSpec((tm, TOP_K), lambda i: (base + i, 0))]
                + [y_spec(k) for k in range(TOP_K)] * 2
                + [full((d, 2 * D_SHARED)), full((D_SHARED, d)), full((1, d)), full((1, d))])
    args = [h, w_t, *([yga] * TOP_K), *([ygb] * TOP_K), wsgu, wsd, g, b]
    aliases = {}
    if prev is not None:
        in_specs.append(pl.BlockSpec(memory_space=pl.ANY))
        aliases = {len(args): 0}
        args.append(prev)
    return pl.pallas_call(
        _combine_kernel,
        grid=(nt,),
        in_specs=in_specs,
        out_specs=pl.BlockSpec((tm, d), lambda i: (base + i, 0)),
        out_shape=jax.ShapeDtypeStruct((t, d), F32),
        input_output_aliases=aliases,
        compiler_params=_cparams(("arbitrary",), 40),
        name="combine",
    )(*args)


def _block_diag4(w):
    n = w.shape[0] // 4
    w4 = w.reshape(n, 4, LRU_BLOCK_DIM, LRU_BLOCK_DIM)
    eye = jnp.eye(4, dtype=w.dtype)
    return jnp.einsum('gaij,ab->gaibj', w4, eye).reshape(n, 4 * LRU_BLOCK_DIM, 4 * LRU_BLOCK_DIM)


def _layer(x, w_in, conv_w, conv_b, lru_w_a, lru_b_a, lru_w_x, lru_b_x, lru_lambda, w_lru_out,
           gla_w_gate_up, gla_b_gate, gla_norm_g, w_gla_out, w_mix_out, ln1_g, ln1_b,
           w_router, router_bias, w_exp_gate, w_exp_up, w_exp_down, w_sh_gate, w_sh_up, w_sh_down,
           ln2_g, ln2_b):
    batch, seq, d = x.shape
    t = batch * seq
    x2 = x.reshape(t, d)
    row = lambda v: v.reshape(1, -1)

    glow_lo = 2 * D_LRU + 2 * GLA_QK_DIM + 2 * GLA_V_DIM
    glow_hi = glow_lo + GLA_GATE_RANK
    w_p = jnp.concatenate(
        [w_in[:, :glow_lo], w_in[:, glow_hi:], w_in[:, glow_lo:glow_hi],
         jnp.zeros((d, GLOW_PAD - GLA_GATE_RANK), w_in.dtype)], axis=1).astype(BF16)
    p = _proj(x2, w_p, tm=min(512, t))

    wg = jnp.concatenate([_block_diag4(lru_w_a), _block_diag4(lru_w_x)], axis=2).astype(BF16)
    ts_lru = min(512, seq)
    lru_o = _lru(p, conv_w, row(conv_b), wg, row(lru_b_a), row(lru_b_x), row(lru_lambda),
                 batch, seq, ts_lru)

    wup = jnp.concatenate(
        [gla_w_gate_up, jnp.zeros((GLOW_PAD - GLA_GATE_RANK, GLA_QK_DIM), gla_w_gate_up.dtype)],
        axis=0).astype(BF16)
    tri = jnp.tril(jnp.ones((GLA_CHUNK, GLA_CHUNK), BF16))
    ts_gla = min(512, seq)
    gla_o = _gla(p, wup, row(gla_b_gate), row(gla_norm_g), tri, batch, seq, ts_gla)

    h, hpa, hpb = _merge(lru_o, gla_o, p, x2, w_lru_out.astype(BF16), w_gla_out.astype(BF16),
                         w_mix_out.astype(BF16), row(ln1_g), row(ln1_b), tm=min(512, t))

    tr = min(256, t)
    upper = jnp.triu(jnp.ones((tr, tr), F32), k=1).astype(BF16)
    wrt = w_router.T
    wrt_top = lax.bitcast_convert_type(
        lax.bitcast_convert_type(wrt, U32) & jnp.uint32(0xFFFF0000), F32)
    wrt_hi = wrt_top.astype(BF16)
    wrt_lo = (wrt - wrt_top).astype(BF16)
    idx, w, rank, cnt = _route(h, jnp.concatenate([wrt_hi, wrt_lo], axis=0),
                               router_bias.reshape(-1, 1), upper, tr)

    counts = cnt[:, 0].astype(I32)
    padded = (counts + EXPERT_ROWS - 1) // EXPERT_ROWS * EXPERT_ROWS
    pend = jnp.cumsum(padded)
    pstart = pend - padded
    dest = _slots(idx, rank, pstart.astype(F32).reshape(-1, 1), tl=min(512, t))
    n_blocks = (t * TOP_K) // EXPERT_ROWS + N_EXPERTS
    n_slots = n_blocks * EXPERT_ROWS
    blk = jnp.arange(n_blocks, dtype=I32)
    n_used = pend[-1] // EXPERT_ROWS
    bmap = jnp.minimum(blk, n_used - 1)
    bexp = jnp.minimum(
        jnp.sum((pend[None, :] <= (bmap * EXPERT_ROWS)[:, None]).astype(I32), axis=1),
        N_EXPERTS - 1)
    first = jnp.concatenate([jnp.ones((1,), I32), (bexp[1:] != bexp[:-1]).astype(I32)])
    flags = (blk < n_used).astype(I32) + 2 * first

    live_end = jnp.sum(jnp.where(bexp[:, None] == jnp.arange(N_EXPERTS, dtype=I32)[None, :],
                                 (pstart + counts)[None, :], 0), axis=1)
    nrow = jnp.where(blk < n_used,
                     jnp.clip(live_end - bmap * EXPERT_ROWS, 0, EXPERT_ROWS), 0).astype(I32)

    xa = _sc_scatter_rows(hpa, dest, n_slots)
    xb = _sc_scatter_rows(hpb, dest, n_slots)
    ya, yb = _experts(n_used, bmap, bexp, flags, nrow, xa, xb, w_exp_gate, w_exp_up, w_exp_down)

    wsgu = jnp.concatenate([w_sh_gate, w_sh_up], axis=1).astype(BF16)
    wsd = w_sh_down.astype(BF16)
    w_t = w.T
    tm_c = min(512, t)
    n_chunks = COMBINE_CHUNKS if t % (COMBINE_CHUNKS * tm_c) == 0 else 1
    tc = t // n_chunks
    out = None
    for c in range(n_chunks):
        idx_c = dest[:, c * tc:(c + 1) * tc].reshape(-1)
        out = _combine(h, w_t, _sc_gather_rows(ya, idx_c), _sc_gather_rows(yb, idx_c), wsgu, wsd,
                       row(ln2_g), row(ln2_b), tm_c, c, n_chunks, out)
    return out.reshape(batch, seq, d)


def kernel(x, w_in, conv_w, conv_b, lru_w_a, lru_b_a, lru_w_x, lru_b_x, lru_lambda, w_lru_out,
           gla_w_gate_up, gla_b_gate, gla_norm_g, w_gla_out, w_mix_out, ln1_g, ln1_b,
           w_router, router_bias, w_exp_gate, w_exp_up, w_exp_down, w_sh_gate, w_sh_up, w_sh_down,
           ln2_g, ln2_b):
    params = (w_in, conv_w, conv_b, lru_w_a, lru_b_a, lru_w_x, lru_b_x, lru_lambda, w_lru_out,
              gla_w_gate_up, gla_b_gate, gla_norm_g, w_gla_out, w_mix_out, ln1_g, ln1_b,
              w_router, router_bias, w_exp_gate, w_exp_up, w_exp_down, w_sh_gate, w_sh_up,
              w_sh_down, ln2_g, ln2_b)
    h = x
    for l in range(DEPTH):
        h = _layer(h, *(p[l] for p in params))
    return h
```
